```python
import math
import jax, jax.numpy as jnp
from jax import lax
import numpy as np

D_MODEL = 1024
BATCH = 2
SEQ = 8192
DEPTH = 2
DEC_BATCH = 128
DEC_SEQ = 4
PAST_LEN = 16384
PAGE_SIZE = 128

H_A = 4
DK_A = 128
DV_A = 128
HGRN_CHUNK = 64
H_B = 8
DH_B = 64
H_C = 16
QK_NOPE = 64
QK_ROPE = 32
DV_C = 64
KV_LORA = 256
Q_LORA = 384
ROPE_THETA = 10000.0
D_FF = 2816
CONV_W = 3
Q_BLOCK = 128
EPS = 1e-6
N_EVEN = (DEPTH + 1) // 2
N_ODD = DEPTH // 2
EVEN_WIDTHS = (H_A * DK_A, H_A * DK_A, H_A * DV_A, H_A * DV_A, H_B * DH_B, H_B * DH_B, H_B * DH_B)
D_IN_EVEN = 2 * H_A * DK_A + 2 * H_A * DV_A + 3 * H_B * DH_B
D_IN_ODD = Q_LORA + KV_LORA + QK_ROPE
F32 = jnp.float32

kernel_name = 'hgrn2_stickbreak_mla_convffn_step'


def rmsnorm(x, g):
    xf = x.astype(F32)
    y = xf * lax.rsqrt(jnp.mean(xf * xf, axis=-1, keepdims=True) + EPS)
    return (y * g.astype(F32)).astype(x.dtype)


def rope(x, pos):
    r = x.shape[-1]
    half = r // 2
    freqs = jnp.exp(-math.log(ROPE_THETA) * jnp.arange(half, dtype=F32) * (2.0 / r))
    ang = pos.astype(F32)[:, None] * freqs[None, :]
    shape = (1, pos.shape[0]) + (1,) * (x.ndim - 3) + (half,)
    cos = jnp.cos(ang).reshape(shape)
    sin = jnp.sin(ang).reshape(shape)
    xf = x.astype(F32)
    x1, x2 = xf[..., :half], xf[..., half:]
    return jnp.concatenate([x1 * cos - x2 * sin, x1 * sin + x2 * cos], axis=-1).astype(x.dtype)


def q_block_size(length):
    return Q_BLOCK if length % Q_BLOCK == 0 else length


def gather_pages(cache, page_table):
    g = cache[page_table]
    return g.reshape((page_table.shape[0], page_table.shape[1] * cache.shape[1]) + cache.shape[2:])


def hgrn2_recurrence(q, k, logf, v, s0):
    b_, l_, h_, dk = q.shape
    dv = v.shape[-1]
    c = HGRN_CHUNK if l_ % HGRN_CHUNK == 0 else l_
    n = l_ // c

    def to_chunks(t):
        return t.astype(F32).reshape(b_, n, c, h_, t.shape[-1]).transpose(1, 0, 3, 2, 4)

    causal = jnp.tril(jnp.ones((c, c), dtype=bool))[None, None, :, :, None]

    def step(s, inp):
        qc, kc, gc, vc = inp
        bcum = jnp.cumsum(gc, axis=2)
        o_inter = jnp.einsum('bhtk,bhkv->bhtv', qc * jnp.exp(bcum), s)
        diff = jnp.where(causal, bcum[:, :, :, None, :] - bcum[:, :, None, :, :], -jnp.inf)
        att = jnp.einsum('bhtk,bhtsk,bhsk->bhts', qc, jnp.exp(diff), kc)
        o = o_inter + jnp.einsum('bhts,bhsv->bhtv', att, vc)
        b_last = bcum[:, :, -1:, :]
        s = jnp.exp(b_last[:, :, 0, :, None]) * s + jnp.einsum('bhsk,bhsv->bhkv', kc * jnp.exp(b_last - bcum), vc)
        return s, o

    s, o = lax.scan(step, s0.astype(F32), (to_chunks(q), to_chunks(k), to_chunks(logf), to_chunks(v)))
    o = o.transpose(1, 0, 3, 2, 4).reshape(b_, l_, h_, dv)
    return o, s


def stick_breaking(q, k, v, q_pos, k_pos):
    b_, lq, h_, d = q.shape
    qb = q_block_size(lq)
    n = lq // qb
    scale = d ** -0.5
    vf = v.astype(F32)
    q_blocks = q.reshape(b_, n, qb, h_, d).transpose(1, 0, 2, 3, 4)
    p_blocks = q_pos.reshape(n, qb)

    def block(args):
        qblk, pblk = args
        z = jnp.einsum('bqhd,bkhd->bhqk', qblk, k).astype(F32) * scale
        mask = (k_pos[None, :] < pblk[:, None])[None, None]
        lneg = jnp.where(mask, jax.nn.log_sigmoid(-z), 0.0)
        between = lax.cumsum(lneg, axis=3, reverse=True) - lneg
        w = jnp.where(mask, jnp.exp(jax.nn.log_sigmoid(z) + between), 0.0)
        return jnp.einsum('bhqk,bkhd->bqhd', w, vf)

    o = lax.map(block, (q_blocks, p_blocks))
    return o.transpose(1, 0, 2, 3, 4).reshape(b_, lq, h_, d)


def mla_attend(q_lat, q_rope, ckv, krope, q_pos, k_pos):
    b_, lq, h_, c = q_lat.shape
    qb = q_block_size(lq)
    n = lq // qb
    scale = (QK_NOPE + QK_ROPE) ** -0.5
    ckvf = ckv.astype(F32)
    ql_blocks = q_lat.reshape(b_, n, qb, h_, c).transpose(1, 0, 2, 3, 4)
    qr_blocks = q_rope.reshape(b_, n, qb, h_, QK_ROPE).transpose(1, 0, 2, 3, 4)
    p_blocks = q_pos.reshape(n, qb)

    def block(args):
        ql, qr, pblk = args
        s = (jnp.einsum('bqhc,bkc->bhqk', ql, ckv) + jnp.einsum('bqhr,bkr->bhqk', qr, krope)).astype(F32) * scale
        mask = (k_pos[None, :] <= pblk[:, None])[None, None]
        p = jax.nn.softmax(jnp.where(mask, s, -jnp.inf), axis=-1)
        return jnp.einsum('bhqk,bkc->bqhc', p, ckvf)

    o = lax.map(block, (ql_blocks, qr_blocks, p_blocks))
    return o.transpose(1, 0, 2, 3, 4).reshape(b_, lq, h_, c)


def even_mixer(h, q_pos, s0, k_past, v_past, w_in, w_out, lb, g_gain):
    b_, l_, _ = h.shape
    cuts = [int(c) for c in np.cumsum(EVEN_WIDTHS)[:-1]]
    qa, fa, ia, ga, qb, kb, vb = jnp.split(h @ w_in, cuts, axis=-1)
    f = lb + (1.0 - lb) * jax.nn.sigmoid(fa.astype(F32))
    heads = lambda t, d: t.reshape(b_, l_, -1, d)
    oa, s = hgrn2_recurrence(heads(qa, DK_A), heads(1.0 - f, DK_A), heads(jnp.log(f), DK_A), heads(ia, DV_A), s0)
    oa = rmsnorm(oa, g_gain) * jax.nn.silu(heads(ga, DV_A).astype(F32))
    qb_h, kb_h, vb_h = heads(qb, DH_B), heads(kb, DH_B), heads(vb, DH_B)
    if k_past is None:
        k_all, v_all, k_pos = kb_h, vb_h, q_pos
    else:
        k_all = jnp.concatenate([k_past.astype(kb_h.dtype), kb_h], axis=1)
        v_all = jnp.concatenate([v_past.astype(vb_h.dtype), vb_h], axis=1)
        k_pos = jnp.arange(k_all.shape[1])
    ob = stick_breaking(qb_h, k_all, v_all, q_pos, k_pos)
    o = jnp.concatenate([oa.reshape(b_, l_, -1), ob.reshape(b_, l_, -1)], axis=-1).astype(h.dtype) @ w_out
    return o, s, kb_h, vb_h


def odd_mixer(h, q_pos, lat_past, rope_past, w_in, qn_g, kvn_g, w_uq, w_uk, w_uv, w_out):
    b_, l_, _ = h.shape
    cq, ckv, kr = jnp.split(h @ w_in, [Q_LORA, Q_LORA + KV_LORA], axis=-1)
    q = (rmsnorm(cq, qn_g) @ w_uq).reshape(b_, l_, H_C, QK_NOPE + QK_ROPE)
    q_nope = q[..., :QK_NOPE]
    q_rope = rope(q[..., QK_NOPE:], q_pos)
    ckv = rmsnorm(ckv, kvn_g)
    kr = rope(kr, q_pos)
    q_lat = jnp.einsum('bqhn,chn->bqhc', q_nope, w_uk)
    if lat_past is None:
        ckv_all, kr_all, k_pos = ckv, kr, q_pos
    else:
        ckv_all = jnp.concatenate([lat_past.astype(ckv.dtype), ckv], axis=1)
        kr_all = jnp.concatenate([rope_past.astype(kr.dtype), kr], axis=1)
        k_pos = jnp.arange(ckv_all.shape[1])
    o_lat = mla_attend(q_lat, q_rope, ckv_all, kr_all, q_pos, k_pos)
    o = jnp.einsum('bqhc,chv->bqhv', o_lat, w_uv).reshape(b_, l_, H_C * DV_C).astype(h.dtype) @ w_out
    return o, ckv, kr


def conv_ffn(h, buf, w_in, conv_w, conv_b, w_out):
    b_, l_, _ = h.shape
    a, b = jnp.split(h @ w_in, [D_FF], axis=-1)
    if buf is None:
        buf = jnp.zeros((b_, CONV_W - 1, D_FF), a.dtype)
    a_ext = jnp.concatenate([buf.astype(a.dtype), a], axis=1)
    c = conv_b + sum(conv_w[i] * a_ext[:, i:i + l_] for i in range(CONV_W))
    y = (jax.nn.silu(c) * b) @ w_out
    return y, a_ext[:, l_:]


def setup_inputs(seed: int = 0) -> dict:
    key = jax.random.key(seed)
    ks = jax.random.split(key, 32)
    n_pages = PAST_LEN // PAGE_SIZE
    n_used = DEC_BATCH * n_pages
    n_pool = (n_used * 5) // 4
    nrm = lambda k, shape, scale: jax.random.normal(k, shape, F32) * scale
    page_table = jax.random.permutation(ks[0], n_pool)[:n_used].reshape(DEC_BATCH, n_pages).astype(jnp.int32)
    return {
        'x_prompt': nrm(ks[1], (BATCH, SEQ, D_MODEL), 1.0),
        'x_sample': nrm(ks[2], (DEC_BATCH, DEC_SEQ, D_MODEL), 1.0),
        'cache_sb_k': nrm(ks[3], (N_EVEN, n_pool, PAGE_SIZE, H_B, DH_B), 1.0),
        'cache_sb_v': nrm(ks[4], (N_EVEN, n_pool, PAGE_SIZE, H_B, DH_B), 1.0),
        'cache_mla_latent': nrm(ks[5], (N_ODD, n_pool, PAGE_SIZE, KV_LORA), 1.0),
        'cache_mla_rope': nrm(ks[6], (N_ODD, n_pool, PAGE_SIZE, QK_ROPE), 1.0),
        'state_hgrn': nrm(ks[7], (N_EVEN, DEC_BATCH, H_A, DK_A, DV_A), 0.5),
        'state_conv': nrm(ks[8], (DEPTH, DEC_BATCH, CONV_W - 1, D_FF), 1.0),
        'page_table': page_table,
        'w_in_even': nrm(ks[9], (N_EVEN, D_MODEL, D_IN_EVEN), D_MODEL ** -0.5),
        'w_out_even': nrm(ks[10], (N_EVEN, H_A * DV_A + H_B * DH_B, D_MODEL), (H_A * DV_A + H_B * DH_B) ** -0.5),
        'hgrn_lb': nrm(ks[11], (N_EVEN + 1, H_A * DK_A), 0.5),
        'hgrn_gnorm': 1.0 + nrm(ks[12], (N_EVEN, DV_A), 0.02),
        'w_in_odd': nrm(ks[13], (N_ODD, D_MODEL, D_IN_ODD), D_MODEL ** -0.5),
        'mla_q_norm': 1.0 + nrm(ks[14], (N_ODD, Q_LORA), 0.02),
        'mla_kv_norm': 1.0 + nrm(ks[15], (N_ODD, KV_LORA), 0.02),
        'w_uq': nrm(ks[16], (N_ODD, Q_LORA, H_C * (QK_NOPE + QK_ROPE)), Q_LORA ** -0.5),
        'w_uk': nrm(ks[17], (N_ODD, KV_LORA, H_C, QK_NOPE), KV_LORA ** -0.5),
        'w_uv': nrm(ks[18], (N_ODD, KV_LORA, H_C, DV_C), KV_LORA ** -0.5),
        'w_out_odd': nrm(ks[19], (N_ODD, H_C * DV_C, D_MODEL), (H_C * DV_C) ** -0.5),
        'norm_mix': 1.0 + nrm(ks[20], (DEPTH, D_MODEL), 0.02),
        'norm_ffn': 1.0 + nrm(ks[21], (DEPTH, D_MODEL), 0.02),
        'w_ffn_in': nrm(ks[22], (DEPTH, D_MODEL, 2 * D_FF), D_MODEL ** -0.5),
        'ffn_conv_w': nrm(ks[23], (DEPTH, CONV_W, D_FF), CONV_W ** -0.5),
        'ffn_conv_b': nrm(ks[24], (DEPTH, D_FF), 0.01),
        'w_ffn_out': nrm(ks[25], (DEPTH, D_FF, D_MODEL), D_FF ** -0.5),
        'norm_final': 1.0 + nrm(ks[26], (D_MODEL,), 0.02),
    }


def reference(x_prompt, x_sample, cache_sb_k, cache_sb_v, cache_mla_latent, cache_mla_rope, state_hgrn, state_conv, page_table,
              w_in_even, w_out_even, hgrn_lb, hgrn_gnorm, w_in_odd, mla_q_norm, mla_kv_norm, w_uq, w_uk, w_uv, w_out_odd,
              norm_mix, norm_ffn, w_ffn_in, ffn_conv_w, ffn_conv_b, w_ffn_out, norm_final):
    past_len = page_table.shape[1] * cache_sb_k.shape[2]
    pos_p = jnp.arange(x_prompt.shape[1])
    pos_s = past_len + jnp.arange(x_sample.shape[1])
    lb_all = jnp.cumsum(jax.nn.softmax(hgrn_lb.astype(F32), axis=0), axis=0)
    xp, xs = x_prompt, x_sample
    sbk_p, sbv_p, sbk_s, sbv_s, hg_p, hg_s = [], [], [], [], [], []
    lat_p, rop_p, lat_s, rop_s, cv_p, cv_s = [], [], [], [], [], []
    for l in range(DEPTH):
        j = l // 2
        hp = rmsnorm(xp, norm_mix[l])
        hs = rmsnorm(xs, norm_mix[l])
        if l % 2 == 0:
            s0p = jnp.zeros((xp.shape[0], H_A, DK_A, DV_A), F32)
            op, sp, kp, vp = even_mixer(hp, pos_p, s0p, None, None, w_in_even[j], w_out_even[j], lb_all[j], hgrn_gnorm[j])
            os_, ss, ksm, vsm = even_mixer(hs, pos_s, state_hgrn[j], gather_pages(cache_sb_k[j], page_table),
                                          gather_pages(cache_sb_v[j], page_table), w_in_even[j], w_out_even[j], lb_all[j], hgrn_gnorm[j])
            sbk_p.append(kp)
            sbv_p.append(vp)
            sbk_s.append(ksm)
            sbv_s.append(vsm)
            hg_p.append(sp.astype(state_hgrn.dtype))
            hg_s.append(ss.astype(state_hgrn.dtype))
        else:
            op, cp, rp = odd_mixer(hp, pos_p, None, None, w_in_odd[j], mla_q_norm[j], mla_kv_norm[j], w_uq[j], w_uk[j], w_uv[j], w_out_odd[j])
            os_, cs, rs = odd_mixer(hs, pos_s, gather_pages(cache_mla_latent[j], page_table), gather_pages(cache_mla_rope[j], page_table),
                                    w_in_odd[j], mla_q_norm[j], mla_kv_norm[j], w_uq[j], w_uk[j], w_uv[j], w_out_odd[j])
            lat_p.append(cp)
            rop_p.append(rp)
            lat_s.append(cs)
            rop_s.append(rs)
        xp = xp + op.astype(xp.dtype)
        xs = xs + os_.astype(xs.dtype)
        fp, bp = conv_ffn(rmsnorm(xp, norm_ffn[l]), None, w_ffn_in[l], ffn_conv_w[l], ffn_conv_b[l], w_ffn_out[l])
        fs, bs = conv_ffn(rmsnorm(xs, norm_ffn[l]), state_conv[l], w_ffn_in[l], ffn_conv_w[l], ffn_conv_b[l], w_ffn_out[l])
        xp = xp + fp.astype(xp.dtype)
        xs = xs + fs.astype(xs.dtype)
        cv_p.append(bp)
        cv_s.append(bs)
    y_prompt = rmsnorm(xp, norm_final)
    y_sample = rmsnorm(xs, norm_final)
    return (y_prompt, y_sample,
            jnp.stack(sbk_p), jnp.stack(sbv_p), jnp.stack(sbk_s), jnp.stack(sbv_s),
            jnp.stack(hg_p), jnp.stack(hg_s),
            jnp.stack(lat_p), jnp.stack(rop_p), jnp.stack(lat_s), jnp.stack(rop_s),
            jnp.stack(cv_p), jnp.stack(cv_s))
```

```python
import functools
import math

import jax
import jax.numpy as jnp
from jax import lax
from jax.experimental import pallas as pl
from jax.experimental.pallas import tpu as pltpu

F32 = jnp.float32
BF16 = jnp.bfloat16

EPS = 1e-6
ROPE_THETA = 10000.0

H_A, DK_A, DV_A = 4, 128, 128
H_B, DH_B = 8, 64
H_C, QK_NOPE, QK_ROPE, DV_C = 16, 64, 32, 64
KV_LORA, Q_LORA = 256, 384
CONV_W = 3

VMEM_LIMIT_BYTES = 56 * 1024 * 1024
LANES = 128
BF16_ROWS = 16

SB_LOG_ZERO = -104.0


def _cparams(*sem):
    return pltpu.CompilerParams(dimension_semantics=sem, vmem_limit_bytes=VMEM_LIMIT_BYTES)


def _rms(x, g):
    return x * lax.rsqrt(jnp.mean(x * x, axis=-1, keepdims=True) + EPS) * g


def _split3(x):
    hi = x.astype(BF16)
    r = x - hi.astype(F32)
    mid = r.astype(BF16)
    lo = (r - mid.astype(F32)).astype(BF16)
    return hi, mid, lo


def _softplus(z):
    return jnp.maximum(z, 0.0) + jnp.log1p(jnp.exp(-jnp.abs(z)))


def _norm_proj_kernel(x_ref, g_ref, w_ref, *out_refs, outs):
    h = _rms(x_ref[...], g_ref[...]).astype(BF16)
    for o_ref, (off, width, _) in zip(out_refs, outs):
        o_ref[...] = jnp.dot(h, w_ref[:, off:off + width], preferred_element_type=F32).astype(o_ref.dtype)


def norm_proj(x, gain, w, outs, tm):
    m, d = x.shape
    n = w.shape[1]
    return pl.pallas_call(
        functools.partial(_norm_proj_kernel, outs=outs),
        grid=(m // tm,),
        in_specs=[pl.BlockSpec((tm, d), lambda i: (i, 0)),
                  pl.BlockSpec((1, d), lambda i: (0, 0)),
                  pl.BlockSpec((d, n), lambda i: (0, 0))],
        out_specs=[pl.BlockSpec((tm, wd), lambda i: (i, 0)) for _, wd, _ in outs],
        out_shape=[jax.ShapeDtypeStruct((m, wd), dt) for _, wd, dt in outs],
        compiler_params=_cparams("parallel"),
        name="norm_proj",
    )(x, gain.reshape(1, d), w)


def _proj_res_kernel(*refs, n_in):
    a_refs = refs[:n_in]
    w_ref, x_ref, o_ref = refs[n_in:]
    acc = x_ref[...]
    off = 0
    for a_ref in a_refs:
        k = a_ref.shape[1]
        acc = acc + jnp.dot(a_ref[...], w_ref[off:off + k, :], preferred_element_type=F32)
        off += k
    o_ref[...] = acc


def proj_res(a_list, w, x, tm):
    m, d = x.shape
    k = w.shape[0]
    return pl.pallas_call(
        functools.partial(_proj_res_kernel, n_in=len(a_list)),
        grid=(m // tm,),
        in_specs=[pl.BlockSpec((tm, a.shape[1]), lambda i: (i, 0)) for a in a_list]
        + [pl.BlockSpec((k, d), lambda i: (0, 0)), pl.BlockSpec((tm, d), lambda i: (i, 0))],
        out_specs=pl.BlockSpec((tm, d), lambda i: (i, 0)),
        out_shape=jax.ShapeDtypeStruct((m, d), F32),
        compiler_params=_cparams("parallel"),
        name="proj_res",
    )(*a_list, w, x)


def _group_mm_kernel(x_ref, w_ref, o_ref):
    o_ref[...] = jnp.dot(x_ref[...].astype(BF16), w_ref[0], preferred_element_type=F32).astype(o_ref.dtype)


def group_mm(x, w, out_dtype):
    m = x.shape[0]
    p, kin, kout = w.shape
    return pl.pallas_call(
        _group_mm_kernel,
        grid=(p,),
        in_specs=[pl.BlockSpec((m, kin), lambda i: (0, i)), pl.BlockSpec((1, kin, kout), lambda i: (i, 0, 0))],
        out_specs=pl.BlockSpec((m, kout), lambda i: (0, i)),
        out_shape=jax.ShapeDtypeStruct((m, p * kout), out_dtype),
        compiler_params=_cparams("parallel"),
        name="group_mm",
    )(x, w)


def _ffn_kernel(*refs, tm, seq_len, has_state, final_norm, a_rows):
    it = iter(refs)
    x_ref, xp_ref, g_ref, wa_ref, wb_ref, cw_ref, cb_ref, wo_ref = (next(it) for _ in range(8))
    e1_ref = next(it) if has_state else None
    e2_ref = next(it) if has_state else None
    gf_ref = next(it) if final_norm else None
    y_ref, a_ref = next(it), next(it)
    h_scr, acc_scr, a_scr = next(it), next(it), next(it)
    i = pl.program_id(0)
    j = pl.program_id(1)
    pad = BF16_ROWS

    @pl.when(j == 0)
    def _():
        h_scr[0:pad, :] = _rms(xp_ref[...], g_ref[...]).astype(BF16)
        h_scr[pad:, :] = _rms(x_ref[...], g_ref[...]).astype(BF16)
        acc_scr[...] = jnp.zeros_like(acc_scr)

    a_scr[...] = jnp.dot(h_scr[...], wa_ref[...], preferred_element_type=F32)
    b = jnp.dot(h_scr[pad:, :], wb_ref[...], preferred_element_type=F32)
    a0 = a_scr[pad:, :]
    a1 = a_scr[pad - 1:pad - 1 + tm, :]
    a2 = a_scr[pad - 2:pad - 2 + tm, :]
    t = lax.rem(i * tm + lax.broadcasted_iota(jnp.int32, (tm, 1), 0), seq_len)
    if has_state:
        s1 = jnp.where(t >= 1, a1, e1_ref[...])
        s2 = jnp.where(t >= 2, a2, e2_ref[...])
    else:
        s1 = jnp.where(t >= 1, a1, 0.0)
        s2 = jnp.where(t >= 2, a2, 0.0)
    cw = cw_ref[...]
    c = cb_ref[...] + (cw[0:1, :] * s2 + cw[1:2, :] * s1 + cw[2:3, :] * a0)
    gate = (c * jax.nn.sigmoid(c)) * b
    acc_scr[...] += jnp.dot(gate.astype(BF16), wo_ref[...], preferred_element_type=F32)
    a_ref[...] = a_scr[pad + tm - a_rows:, :]

    @pl.when(j == pl.num_programs(1) - 1)
    def _():
        y = x_ref[...] + acc_scr[...]
        if final_norm:
            y = _rms(y, gf_ref[...])
        y_ref[...] = y


def conv_ffn(x, gain, w_in, conv_w, conv_b, w_out, *, seq_len, tm, tn, state=None, final_gain=None):
    m, d = x.shape
    d_ff = w_out.shape[0]
    has_state = state is not None
    final_norm = final_gain is not None
    a_rows = tm if has_state else 8
    pad = BF16_ROWS
    nj = d_ff // tn
    tpb = tm // pad
    args = [x, x, gain.reshape(1, d), w_in, w_in, conv_w, conv_b.reshape(1, d_ff), w_out]
    in_specs = [
        pl.BlockSpec((tm, d), lambda i, j: (i, 0)),
        pl.BlockSpec((pad, d), lambda i, j: (jnp.maximum(i * tpb - 1, 0), 0)),
        pl.BlockSpec((1, d), lambda i, j: (0, 0)),
        pl.BlockSpec((d, tn), lambda i, j: (0, j)),
        pl.BlockSpec((d, tn), lambda i, j: (0, nj + j)),
        pl.BlockSpec((CONV_W, tn), lambda i, j: (0, j)),
        pl.BlockSpec((1, tn), lambda i, j: (0, j)),
        pl.BlockSpec((tn, d), lambda i, j: (j, 0)),
    ]
    if has_state:
        b = state.shape[0]
        e1 = jnp.broadcast_to(state[:, 1:2, :], (b, seq_len, d_ff)).reshape(m, d_ff)
        e2 = jnp.tile(state, (1, seq_len // 2, 1)).reshape(m, d_ff)
        args += [e1, e2]
        in_specs += [pl.BlockSpec((tm, tn), lambda i, j: (i, j))] * 2
    if final_norm:
        args.append(final_gain.reshape(1, d))
        in_specs.append(pl.BlockSpec((1, d), lambda i, j: (0, 0)))
    y, a_tail = pl.pallas_call(
        functools.partial(_ffn_kernel, tm=tm, seq_len=seq_len, has_state=has_state,
                          final_norm=final_norm, a_rows=a_rows),
        grid=(m // tm, nj),
        in_specs=in_specs,
        out_specs=[pl.BlockSpec((tm, d), lambda i, j: (i, 0)),
                   pl.BlockSpec((a_rows, tn), lambda i, j: (i, j))],
        out_shape=[jax.ShapeDtypeStruct((m, d), F32),
                   jax.ShapeDtypeStruct((m // tm * a_rows, d_ff), F32)],
        scratch_shapes=[pltpu.VMEM((tm + pad, d), BF16),
                        pltpu.VMEM((tm, d), F32),
                        pltpu.VMEM((tm + pad, tn), F32)],
        compiler_params=_cparams("parallel", "arbitrary"),
        name="conv_ffn",
    )(*args)
    return y, a_tail


def _hgrn_kernel(*refs, chunk, sub, has_s0):
    it = iter(refs)
    q_ref, f_ref, v_ref, g_ref, lb_ref, gn_ref = (next(it) for _ in range(6))
    s0_ref = next(it) if has_s0 else None
    o_ref, s_ref, st_scr = next(it), next(it), next(it)
    c = pl.program_id(1)
    n_sub = chunk // sub

    @pl.when(c == 0)
    def _():
        for h in range(H_A):
            if has_s0:
                st_scr[h] = s0_ref[0, h].T
            else:
                st_scr[h] = jnp.zeros((DV_A, DK_A), F32)

    row = lax.broadcasted_iota(jnp.int32, (chunk, chunk), 0)
    col = lax.broadcasted_iota(jnp.int32, (chunk, chunk), 1)
    tril = jnp.where(col <= row, 1.0, 0.0).astype(BF16)
    tt = lax.broadcasted_iota(jnp.int32, (sub, 1), 0)
    nt_dims = (((1,), (1,)), ((), ()))
    tn_dims = (((0,), (0,)), ((), ()))

    for h in range(H_A):
        sl = slice(h * DK_A, (h + 1) * DK_A)
        q = q_ref[0, :, sl]
        v = v_ref[0, :, sl]
        lb = lb_ref[:, sl]
        f = lb + (1.0 - lb) * jax.nn.sigmoid(f_ref[0, :, sl])
        kk = 1.0 - f
        logf = jnp.log(f)
        bc = sum(jnp.dot(tril, part, preferred_element_type=F32) for part in _split3(logf))
        st = st_scr[h]
        o = lax.dot_general((q * jnp.exp(bc)).astype(BF16), st.astype(BF16), nt_dims,
                            preferred_element_type=F32)
        vb = v.astype(BF16)
        parts = []
        for i in range(n_sub):
            r0 = i * sub
            qi, ki, vi, bi = q[r0:r0 + sub], kk[r0:r0 + sub], v[r0:r0 + sub], bc[r0:r0 + sub]
            oi = o[r0:r0 + sub]
            for s in range(sub):
                p = qi * jnp.exp(bi - bi[s:s + 1]) * ki[s:s + 1]
                att = jnp.where(tt >= s, jnp.sum(p, axis=-1, keepdims=True), 0.0)
                oi = oi + att * vi[s:s + 1]
            if i > 0:
                e = bc[r0 - 1:r0]
                qt = (qi * jnp.exp(bi - e)).astype(BF16)
                kt = (kk[:r0] * jnp.exp(e - bc[:r0])).astype(BF16)
                att = lax.dot_general(qt, kt, nt_dims, preferred_element_type=F32)
                oi = oi + jnp.dot(att.astype(BF16), vb[:r0], preferred_element_type=F32)
            parts.append(oi)
        o = parts[0] if n_sub == 1 else jnp.concatenate(parts, axis=0)
        bl = bc[chunk - 1:chunk]
        kh = (kk * jnp.exp(bl - bc)).astype(BF16)
        st_scr[h] = st * jnp.exp(bl) + lax.dot_general(vb, kh, tn_dims, preferred_element_type=F32)
        on = _rms(o, gn_ref[...])
        ga = g_ref[0, :, sl]
        o_ref[0, :, sl] = (on * (ga * jax.nn.sigmoid(ga))).astype(o_ref.dtype)

    @pl.when(c == pl.num_programs(1) - 1)
    def _():
        for h in range(H_A):
            s_ref[0, h] = st_scr[h].T


def hgrn2(q, f, v, g, lb, gnorm, s0, *, chunk, sub):
    b, l, w = q.shape
    has_s0 = s0 is not None
    seq_spec = pl.BlockSpec((1, chunk, w), lambda i, c: (i, c, 0))
    st_spec = pl.BlockSpec((1, H_A, DK_A, DV_A), lambda i, c: (i, 0, 0, 0))
    args = [q, f, v, g, lb.reshape(1, w), gnorm.reshape(1, DV_A)]
    in_specs = [seq_spec] * 4 + [pl.BlockSpec((1, w), lambda i, c: (0, 0)),
                                 pl.BlockSpec((1, DV_A), lambda i, c: (0, 0))]
    if has_s0:
        args.append(s0)
        in_specs.append(st_spec)
    return pl.pallas_call(
        functools.partial(_hgrn_kernel, chunk=chunk, sub=sub, has_s0=has_s0),
        grid=(b, l // chunk),
        in_specs=in_specs,
        out_specs=[seq_spec, st_spec],
        out_shape=[jax.ShapeDtypeStruct((b, l, w), BF16),
                   jax.ShapeDtypeStruct((b, H_A, DK_A, DV_A), F32)],
        scratch_shapes=[pltpu.VMEM((H_A, DV_A, DK_A), F32)],
        compiler_params=_cparams("parallel", "arbitrary"),
        name="hgrn2",
    )(*args)


def _sb_block(qm, kb, vm, mask, carry, umat, scale):
    nt_dims = (((1,), (1,)), ((), ()))
    z = lax.dot_general(qm, kb, nt_dims, preferred_element_type=F32) * scale
    lneg = -_softplus(z)
    if mask is not None:
        lneg = jnp.where(mask, lneg, 0.0)
    loc = sum(jnp.dot(part, umat, preferred_element_type=F32) for part in _split3(lneg))
    w = jnp.exp(z + lneg + loc + carry)
    if mask is not None:
        w = jnp.where(mask, w, 0.0)
    out = jnp.dot(w.astype(BF16), vm, preferred_element_type=F32)
    return out, carry + loc[:, 0:1] + lneg[:, 0:1]


def _strict_upper(n):
    r = lax.broadcasted_iota(jnp.int32, (n, n), 0)
    c = lax.broadcasted_iota(jnp.int32, (n, n), 1)
    return jnp.where(r > c, 1.0, 0.0).astype(BF16)


def _sb_prompt_kernel(q_ref, k_ref, v_ref, o_ref, *, blk):
    qi = pl.program_id(2)
    scale = DH_B ** -0.5
    q = q_ref[0]
    lane = lax.broadcasted_iota(jnp.int32, (1, 2 * DH_B), 1)
    head_lanes = [lane < DH_B, lane >= DH_B]
    qs = [jnp.where(m, q, jnp.zeros_like(q)) for m in head_lanes]
    umat = _strict_upper(blk)
    qpos = qi * blk + lax.broadcasted_iota(jnp.int32, (blk, 1), 0)
    kiota = lax.broadcasted_iota(jnp.int32, (1, blk), 1)

    def cond(state):
        j, c0, c1, _ = state
        live = jnp.maximum(jnp.max(c0), jnp.max(c1)) > SB_LOG_ZERO
        return jnp.logical_and(j >= 0, live)

    def body(state):
        j, c0, c1, acc = state
        start = pl.multiple_of(j * blk, blk)
        kb = k_ref[0, pl.ds(start, blk), :]
        vb = v_ref[0, pl.ds(start, blk), :]
        mask = (j * blk + kiota) < qpos
        cs = [c0, c1]
        for h in range(2):
            vm = jnp.where(head_lanes[h], vb, jnp.zeros_like(vb))
            out, cs[h] = _sb_block(qs[h], kb, vm, mask, cs[h], umat, scale)
            acc = acc + out
        return j - 1, cs[0], cs[1], acc

    zero_c = jnp.zeros((blk, 1), F32)
    _, _, _, acc = lax.while_loop(cond, body, (qi, zero_c, zero_c, jnp.zeros((blk, 2 * DH_B), F32)))
    o_ref[0] = acc.astype(o_ref.dtype)


def sb_prompt(q, k, v, *, blk):
    b, l, w = q.shape
    pw = 2 * DH_B
    q_spec = pl.BlockSpec((1, blk, pw), lambda i, p, t: (i, t, p))
    kv_spec = pl.BlockSpec((1, l, pw), lambda i, p, t: (i, 0, p))
    return pl.pallas_call(
        functools.partial(_sb_prompt_kernel, blk=blk),
        grid=(b, w // pw, l // blk),
        in_specs=[q_spec, kv_spec, kv_spec],
        out_specs=q_spec,
        out_shape=jax.ShapeDtypeStruct((b, l, w), BF16),
        compiler_params=_cparams("parallel", "parallel", "arbitrary"),
        name="sb_prompt",
    )(q, k, v)


def _sb_decode_kernel(pt_ref, q_ref, kn_ref, vn_ref, kc_ref, vc_ref, o_ref, kbuf, vbuf, sem, *, n_new, layer):
    b = pl.program_id(0)
    n_pages = pt_ref.shape[1]
    page = kbuf.shape[1]
    w = q_ref.shape[2]
    rows = q_ref.shape[1]
    scale = DH_B ** -0.5

    def copies(j, slot):
        idx = pt_ref[b, j]
        return (pltpu.make_async_copy(kc_ref.at[layer, idx], kbuf.at[slot], sem.at[0, slot]),
                pltpu.make_async_copy(vc_ref.at[layer, idx], vbuf.at[slot], sem.at[1, slot]))

    for cp in copies(n_pages - 1, 0):
        cp.start()

    lane = lax.broadcasted_iota(jnp.int32, (rows, w), 1)
    rix = lax.broadcasted_iota(jnp.int32, (rows, w), 0)
    own = (lane // DH_B) == (rix // n_new)
    q = q_ref[0]
    qm = jnp.where(own, q, jnp.zeros_like(q))
    umat = _strict_upper(page)

    pad_rows = page - kn_ref.shape[1]
    knew = jnp.concatenate([kn_ref[0], jnp.zeros((pad_rows, w), F32)], axis=0).astype(BF16)
    vnew = jnp.concatenate([vn_ref[0], jnp.zeros((pad_rows, w), F32)], axis=0).astype(BF16)
    tq = lax.rem(lax.broadcasted_iota(jnp.int32, (rows, 1), 0), n_new)
    mask_new = lax.broadcasted_iota(jnp.int32, (1, page), 1) < tq
    acc, carry = _sb_block(qm, knew, vnew, mask_new, jnp.zeros((rows, 1), F32), umat, scale)

    def cond(state):
        j, _, carry, _ = state
        return jnp.logical_and(j >= 0, jnp.max(carry) > SB_LOG_ZERO)

    def body(state):
        j, slot, carry, acc = state

        @pl.when(j > 0)
        def _():
            for cp in copies(j - 1, 1 - slot):
                cp.start()

        for cp in copies(j, slot):
            cp.wait()
        out, carry = _sb_block(qm, kbuf[slot].astype(BF16), vbuf[slot].astype(BF16), None, carry, umat, scale)
        return j - 1, 1 - slot, carry, acc + out

    j, slot, _, acc = lax.while_loop(cond, body, (n_pages - 1, 0, carry, acc))

    @pl.when(j >= 0)
    def _():
        for cp in copies(j, slot):
            cp.wait()

    res = jnp.where(own, acc, 0.0)
    o = res[0:n_new]
    for h in range(1, rows // n_new):
        o = o + res[h * n_new:(h + 1) * n_new]
    o_ref[0] = o.astype(o_ref.dtype)


def sb_decode(q_rows, k_new, v_new, k_cache, v_cache, page_table, *, layer, n_new):
    b, rows, w = q_rows.shape
    page = k_cache.shape[2]
    grid_spec = pltpu.PrefetchScalarGridSpec(
        num_scalar_prefetch=1,
        grid=(b,),
        in_specs=[pl.BlockSpec((1, rows, w), lambda i, pt: (i, 0, 0)),
                  pl.BlockSpec((1, k_new.shape[1], w), lambda i, pt: (i, 0, 0)),
                  pl.BlockSpec((1, v_new.shape[1], w), lambda i, pt: (i, 0, 0)),
                  pl.BlockSpec(memory_space=pl.ANY),
                  pl.BlockSpec(memory_space=pl.ANY)],
        out_specs=pl.BlockSpec((1, n_new, w), lambda i, pt: (i, 0, 0)),
        scratch_shapes=[pltpu.VMEM((2, page, w), F32), pltpu.VMEM((2, page, w), F32),
                        pltpu.SemaphoreType.DMA((2, 2))],
    )
    return pl.pallas_call(
        functools.partial(_sb_decode_kernel, n_new=n_new, layer=layer),
        grid_spec=grid_spec,
        out_shape=jax.ShapeDtypeStruct((b, n_new, w), BF16),
        compiler_params=_cparams("arbitrary"),
        name="sb_decode",
    )(page_table, q_rows, k_new, v_new, k_cache, v_cache)


def _rope_lanes(x, cos, sin_signed):
    w = x.shape[1]
    lane = lax.broadcasted_iota(jnp.int32, (1, w), 1)
    first = (lane & (QK_ROPE - 1)) < QK_ROPE // 2
    rot = jnp.where(first, pltpu.roll(x, w - QK_ROPE // 2, 1), pltpu.roll(x, QK_ROPE // 2, 1))
    return x * cos + rot * sin_signed


def _mla_q_kernel(cq_ref, g_ref, w_ref, cos_ref, sin_ref, qn_ref, qr_ref):
    n_nope = qn_ref.shape[1]
    h = _rms(cq_ref[...], g_ref[...]).astype(BF16)
    qn_ref[...] = jnp.dot(h, w_ref[:, :n_nope], preferred_element_type=F32).astype(qn_ref.dtype)
    qr = jnp.dot(h, w_ref[:, n_nope:], preferred_element_type=F32)
    reps = qr.shape[1] // LANES
    cos = jnp.concatenate([cos_ref[...]] * reps, axis=1)
    sin = jnp.concatenate([sin_ref[...]] * reps, axis=1)
    qr_ref[...] = _rope_lanes(qr, cos, sin).astype(qr_ref.dtype)


def mla_q(cq, gain, w_uq_perm, cos, sin, tm):
    m, ql = cq.shape
    n = w_uq_perm.shape[1]
    n_nope, n_rope = H_C * QK_NOPE, H_C * QK_ROPE
    tab_tiles = cos.shape[0] // tm
    tab_spec = pl.BlockSpec((tm, LANES), lambda i: (i % tab_tiles, 0))
    return pl.pallas_call(
        _mla_q_kernel,
        grid=(m // tm,),
        in_specs=[pl.BlockSpec((tm, ql), lambda i: (i, 0)), pl.BlockSpec((1, ql), lambda i: (0, 0)),
                  pl.BlockSpec((ql, n), lambda i: (0, 0)), tab_spec, tab_spec],
        out_specs=[pl.BlockSpec((tm, n_nope), lambda i: (i, 0)), pl.BlockSpec((tm, n_rope), lambda i: (i, 0))],
        out_shape=[jax.ShapeDtypeStruct((m, n_nope), BF16), jax.ShapeDtypeStruct((m, n_rope), BF16)],
        compiler_params=_cparams("parallel"),
        name="mla_q",
    )(cq, gain.reshape(1, ql), w_uq_perm, cos, sin)


def _mla_kv_kernel(*refs, with_heads):
    ckv_ref, kr_ref, g_ref = refs[:3]
    if with_heads:
        w_ref, cos_ref, sin_ref, lat_ref, rope_ref, kn_ref, v_ref, kr16_ref = refs[3:]
    else:
        cos_ref, sin_ref, lat_ref, rope_ref = refs[3:]
    lat = _rms(ckv_ref[...], g_ref[...])
    lat_ref[...] = lat
    rope = _rope_lanes(kr_ref[...], cos_ref[...], sin_ref[...])
    rope_ref[...] = rope
    if with_heads:
        n = kn_ref.shape[1]
        lb = lat.astype(BF16)
        kn_ref[...] = jnp.dot(lb, w_ref[:, :n], preferred_element_type=F32).astype(kn_ref.dtype)
        v_ref[...] = jnp.dot(lb, w_ref[:, n:], preferred_element_type=F32).astype(v_ref.dtype)
        kr16_ref[...] = rope.astype(kr16_ref.dtype)


def mla_kv(ckv, kr_rep, gain, w_ukv, cos, sin, tm):
    m, c = ckv.shape
    with_heads = w_ukv is not None
    tab_tiles = cos.shape[0] // tm
    tab_spec = pl.BlockSpec((tm, LANES), lambda i: (i % tab_tiles, 0))
    row = lambda wd: pl.BlockSpec((tm, wd), lambda i: (i, 0))
    args = [ckv, kr_rep, gain.reshape(1, c)]
    in_specs = [row(c), row(LANES), pl.BlockSpec((1, c), lambda i: (0, 0))]
    out_specs = [row(c), row(LANES)]
    out_shape = [jax.ShapeDtypeStruct((m, c), F32), jax.ShapeDtypeStruct((m, LANES), F32)]
    if with_heads:
        n2 = w_ukv.shape[1]
        args.append(w_ukv)
        in_specs.append(pl.BlockSpec((c, n2), lambda i: (0, 0)))
        out_specs += [row(n2 // 2), row(n2 // 2), row(LANES)]
        out_shape += [jax.ShapeDtypeStruct((m, n2 // 2), BF16), jax.ShapeDtypeStruct((m, n2 // 2), BF16),
                      jax.ShapeDtypeStruct((m, LANES), BF16)]
    args += [cos, sin]
    in_specs += [tab_spec, tab_spec]
    return pl.pallas_call(
        functools.partial(_mla_kv_kernel, with_heads=with_heads),
        grid=(m // tm,),
        in_specs=in_specs,
        out_specs=out_specs,
        out_shape=out_shape,
        compiler_params=_cparams("parallel"),
        name="mla_kv",
    )(*args)


NEG_BIG = -1e30


def _mla_prompt_kernel(qn_ref, qr_ref, kn_ref, kr_ref, v_ref, o_ref, *, blk):
    p = pl.program_id(1)
    qi = pl.program_id(2)
    scale = (QK_NOPE + QK_ROPE) ** -0.5
    nt_dims = (((1,), (1,)), ((), ()))
    lane = lax.broadcasted_iota(jnp.int32, (1, LANES), 1)
    qn = qn_ref[0]
    qr = qr_ref[0]
    zero = jnp.zeros_like(qn)
    head_lanes = [(lane >> 6) == hh for hh in range(2)]
    rows = []
    for hh in range(2):
        rope_lanes = (lane >> 5) == lax.rem(2 * p + hh, 4)
        rows.append(jnp.concatenate([jnp.where(head_lanes[hh], qn, zero), jnp.where(rope_lanes, qr, zero)], axis=1))
    qq = jnp.concatenate(rows, axis=0)

    def step(j, carry, diag):
        m, l, acc = carry
        start = pl.multiple_of(j * blk, blk)
        kk = jnp.concatenate([kn_ref[0, pl.ds(start, blk), :], kr_ref[0, pl.ds(start, blk), :]], axis=1)
        vb = v_ref[0, pl.ds(start, blk), :]
        s = lax.dot_general(qq, kk, nt_dims, preferred_element_type=F32) * scale
        if diag:
            r = lax.broadcasted_iota(jnp.int32, (2 * blk, blk), 0)
            c = lax.broadcasted_iota(jnp.int32, (2 * blk, blk), 1)
            s = jnp.where(c <= (r & (blk - 1)), s, NEG_BIG)
        m_new = jnp.maximum(m, jnp.max(s, axis=-1, keepdims=True))
        alpha = jnp.exp(m - m_new)
        pr = jnp.exp(s - m_new)
        l = l * alpha + jnp.sum(pr, axis=-1, keepdims=True)
        pb = pr.astype(BF16)
        zv = jnp.zeros_like(vb)
        pv = (jnp.dot(pb[:blk], jnp.where(head_lanes[0], vb, zv), preferred_element_type=F32)
              + jnp.dot(pb[blk:], jnp.where(head_lanes[1], vb, zv), preferred_element_type=F32))
        acc = acc * jnp.where(head_lanes[0], alpha[:blk], alpha[blk:]) + pv
        return m_new, l, acc

    init = (jnp.full((2 * blk, 1), NEG_BIG, F32), jnp.zeros((2 * blk, 1), F32), jnp.zeros((blk, LANES), F32))
    carry = lax.fori_loop(0, qi, lambda j, cy: step(j, cy, False), init)
    _, l, acc = step(qi, carry, True)
    o_ref[0] = (acc / jnp.where(head_lanes[0], l[:blk], l[blk:])).astype(o_ref.dtype)


def mla_prompt(qn, qr, kn, kr16, v, *, blk):
    b, l, w = qn.shape
    q_spec = pl.BlockSpec((1, blk, LANES), lambda i, p, t: (i, t, p))
    qr_spec = pl.BlockSpec((1, blk, LANES), lambda i, p, t: (i, t, p // 2))
    kv_spec = pl.BlockSpec((1, l, LANES), lambda i, p, t: (i, 0, p))
    kr_spec = pl.BlockSpec((1, l, LANES), lambda i, p, t: (i, 0, 0))
    return pl.pallas_call(
        functools.partial(_mla_prompt_kernel, blk=blk),
        grid=(b, w // LANES, l // blk),
        in_specs=[q_spec, qr_spec, kv_spec, kr_spec, kv_spec],
        out_specs=q_spec,
        out_shape=jax.ShapeDtypeStruct((b, l, w), BF16),
        compiler_params=_cparams("parallel", "parallel", "arbitrary"),
        name="mla_prompt",
    )(qn, qr, kn, kr16, v)


def _mla_decode_kernel(pt_ref, ql_ref, qr_ref, ln_ref, rn_ref, lc_ref, rc_ref, o_ref, lbuf, rbuf, sem,
                       *, group, layer):
    b = pl.program_id(0)
    n_pages = pt_ref.shape[1]
    n_groups = n_pages // group
    page = lbuf.shape[2]
    rows = ql_ref.shape[1]
    scale = (QK_NOPE + QK_ROPE) ** -0.5
    nt_dims = (((1,), (1,)), ((), ()))

    def copies(g, slot):
        cps = []
        for u in range(group):
            idx = pt_ref[b, g * group + u]
            cps.append(pltpu.make_async_copy(lc_ref.at[layer, idx], lbuf.at[slot, u], sem.at[0, slot]))
            cps.append(pltpu.make_async_copy(rc_ref.at[layer, idx], rbuf.at[slot, u], sem.at[1, slot]))
        return cps

    for cp in copies(0, 0):
        cp.start()

    ql = ql_ref[0]
    qr = qr_ref[0]

    def attend(lat, rope, mask, carry):
        m, l, acc = carry
        latb = lat.astype(BF16)
        s = (lax.dot_general(ql, latb, nt_dims, preferred_element_type=F32)
             + lax.dot_general(qr, rope.astype(BF16), nt_dims, preferred_element_type=F32)) * scale
        if mask is not None:
            s = jnp.where(mask, s, NEG_BIG)
        m_new = jnp.maximum(m, jnp.max(s, axis=-1, keepdims=True))
        alpha = jnp.exp(m - m_new)
        pr = jnp.exp(s - m_new)
        l = l * alpha + jnp.sum(pr, axis=-1, keepdims=True)
        acc = acc * alpha + jnp.dot(pr.astype(BF16), latb, preferred_element_type=F32)
        return m_new, l, acc

    n_pad = ln_ref.shape[1]
    tq = lax.broadcasted_iota(jnp.int32, (rows, n_pad), 0) // H_C
    mask_new = lax.broadcasted_iota(jnp.int32, (rows, n_pad), 1) <= tq
    init = (jnp.full((rows, 1), NEG_BIG, F32), jnp.zeros((rows, 1), F32), jnp.zeros((rows, lbuf.shape[3]), F32))
    carry = attend(ln_ref[0], rn_ref[0], mask_new, init)

    def body(g, carry):
        slot = lax.rem(g, 2)

        @pl.when(g + 1 < n_groups)
        def _():
            for cp in copies(g + 1, 1 - slot):
                cp.start()

        for cp in copies(g, slot):
            cp.wait()
        lat = lbuf[slot].reshape(group * page, lbuf.shape[3])
        rope = rbuf[slot].reshape(group * page, rbuf.shape[3])
        return attend(lat, rope, None, carry)

    _, l, acc = lax.fori_loop(0, n_groups, body, carry)
    o_ref[0] = acc / l


def mla_decode(q_lat, q_rope, lat_new, rope_new, lat_cache, rope_cache, page_table, *, layer, group):
    b, rows, c = q_lat.shape
    r = q_rope.shape[2]
    page = lat_cache.shape[2]
    n_pad = lat_new.shape[1]
    grid_spec = pltpu.PrefetchScalarGridSpec(
        num_scalar_prefetch=1,
        grid=(b,),
        in_specs=[pl.BlockSpec((1, rows, c), lambda i, pt: (i, 0, 0)),
                  pl.BlockSpec((1, rows, r), lambda i, pt: (i, 0, 0)),
                  pl.BlockSpec((1, n_pad, c), lambda i, pt: (i, 0, 0)),
                  pl.BlockSpec((1, n_pad, r), lambda i, pt: (i, 0, 0)),
                  pl.BlockSpec(memory_space=pl.ANY),
                  pl.BlockSpec(memory_space=pl.ANY)],
        out_specs=pl.BlockSpec((1, rows, c), lambda i, pt: (i, 0, 0)),
        scratch_shapes=[pltpu.VMEM((2, group, page, c), F32), pltpu.VMEM((2, group, page, r), F32),
                        pltpu.SemaphoreType.DMA((2, 2))],
    )
    return pl.pallas_call(
        functools.partial(_mla_decode_kernel, group=group, layer=layer),
        grid_spec=grid_spec,
        out_shape=jax.ShapeDtypeStruct((b, rows, c), F32),
        compiler_params=_cparams("arbitrary"),
        name="mla_decode",
    )(page_table, q_lat, q_rope, lat_new, rope_new, lat_cache, rope_cache)


TM = 512
TN_FF = 256
HGRN_CHUNK = 64
HGRN_SUB = 16
SB_BLK = 128
MLA_BLK = 256
MLA_PAGE_GROUP = 8


def _row_tile(m):
    return min(TM, m)


def _rope_tables(pos):
    half = QK_ROPE // 2
    freqs = jnp.exp(-math.log(ROPE_THETA) * jnp.arange(half, dtype=F32) * (2.0 / QK_ROPE))
    ang = pos.astype(F32)[:, None] * freqs[None, :]
    cos, sin = jnp.cos(ang), jnp.sin(ang)
    reps = LANES // QK_ROPE
    return (jnp.tile(jnp.concatenate([cos, cos], axis=1), (1, reps)),
            jnp.tile(jnp.concatenate([-sin, sin], axis=1), (1, reps)))


def _even_mixer(x, seq, w_in, w_out, lb, gnorm, gain, *, s0=None, caches=None, page_table=None, layer=0):
    b, l = seq
    wa, wb = H_A * DK_A, H_B * DH_B
    outs = ((0, wa, F32), (wa, wa, F32), (2 * wa, wa, F32), (3 * wa, wa, F32),
            (4 * wa, wb, BF16), (4 * wa + wb, wb, F32), (4 * wa + 2 * wb, wb, F32))
    if caches is None:
        outs += ((4 * wa + wb, wb, BF16), (4 * wa + 2 * wb, wb, BF16))
    tm = _row_tile(b * l)
    res = norm_proj(x, gain, w_in, outs, tm)
    qa, fa, ia, ga, qb, kb, vb = res[:7]
    r3 = lambda t: t.reshape(b, l, t.shape[-1])
    if caches is None:
        oa, state = hgrn2(r3(qa), r3(fa), r3(ia), r3(ga), lb, gnorm, None, chunk=HGRN_CHUNK, sub=HGRN_SUB)
        ob = sb_prompt(r3(qb), r3(res[7]), r3(res[8]), blk=SB_BLK)
    else:
        oa, state = hgrn2(r3(qa), r3(fa), r3(ia), r3(ga), lb, gnorm, s0, chunk=l, sub=l)
        k_cache, v_cache = caches
        pad = ((0, 0), (0, 8 - l), (0, 0))
        ob = sb_decode(jnp.tile(r3(qb), (1, H_B, 1)), jnp.pad(r3(kb), pad), jnp.pad(r3(vb), pad),
                       k_cache.reshape(k_cache.shape[:3] + (wb,)), v_cache.reshape(v_cache.shape[:3] + (wb,)),
                       page_table, layer=layer, n_new=l)
    x = proj_res([oa.reshape(b * l, wa), ob.reshape(b * l, wb)], w_out, x, tm)
    return x, state, kb.reshape(b, l, H_B, DH_B), vb.reshape(b, l, H_B, DH_B)


def _odd_mixer(x, seq, pos, w_in_ext, gain, qn_g, kvn_g, w_uq_perm, w_uk, w_uv, w_out,
               *, caches=None, page_table=None, layer=0):
    b, l = seq
    m = b * l
    tm = _row_tile(m)
    cq, ckv, kr_rep = norm_proj(x, gain, w_in_ext,
                                ((0, Q_LORA, F32), (Q_LORA, KV_LORA, F32), (Q_LORA + KV_LORA, LANES, F32)), tm)
    cos, sin = _rope_tables(pos)
    qn, qr = mla_q(cq, qn_g, w_uq_perm, cos, sin, tm)
    if caches is None:
        w_ukv = jnp.concatenate([w_uk.reshape(KV_LORA, -1), w_uv.reshape(KV_LORA, -1)], axis=1).astype(BF16)
        lat, rope, kn, v, kr16 = mla_kv(ckv, kr_rep, kvn_g, w_ukv, cos, sin, tm)
        r3 = lambda t: t.reshape(b, l, t.shape[-1])
        o = mla_prompt(r3(qn), r3(qr), r3(kn), r3(kr16), r3(v), blk=MLA_BLK).reshape(m, -1)
    else:
        lat, rope = mla_kv(ckv, kr_rep, kvn_g, None, cos, sin, tm)
        eye2 = jnp.eye(2, dtype=F32)
        wk = jnp.transpose(w_uk, (1, 2, 0)).reshape(H_C // 2, 2, QK_NOPE, KV_LORA)
        wk = jnp.einsum('pand,ab->panbd', wk, eye2).reshape(H_C // 2, 2 * QK_NOPE, 2 * KV_LORA).astype(BF16)
        wv = jnp.transpose(w_uv, (1, 0, 2)).reshape(H_C // 2, 2, KV_LORA, DV_C)
        wv = jnp.einsum('pacv,ab->pacbv', wv, eye2).reshape(H_C // 2, 2 * KV_LORA, 2 * DV_C).astype(BF16)
        q_lat = group_mm(qn, wk, BF16).reshape(b, l * H_C, KV_LORA)
        lat_cache, rope_cache = caches
        pad = ((0, 0), (0, 8 - l), (0, 0))
        o_lat = mla_decode(q_lat, qr.reshape(b, l * H_C, QK_ROPE),
                           jnp.pad(lat.reshape(b, l, KV_LORA), pad),
                           jnp.pad(rope[:, :QK_ROPE].reshape(b, l, QK_ROPE), pad),
                           lat_cache, rope_cache, page_table, layer=layer,
                           group=math.gcd(MLA_PAGE_GROUP, page_table.shape[1]))
        o = group_mm(o_lat.reshape(m, H_C * KV_LORA), wv, BF16)
    x = proj_res([o], w_out, x, tm)
    return x, lat.reshape(b, l, KV_LORA), rope[:, :QK_ROPE].reshape(b, l, QK_ROPE)


def kernel(x_prompt, x_sample, cache_sb_k, cache_sb_v, cache_mla_latent, cache_mla_rope, state_hgrn, state_conv, page_table, w_in_even, w_out_even, hgrn_lb, hgrn_gnorm, w_in_odd, mla_q_norm, mla_kv_norm, w_uq, w_uk, w_uv, w_out_odd, norm_mix, norm_ffn, w_ffn_in, ffn_conv_w, ffn_conv_b, w_ffn_out, norm_final):
    bp, lp, d = x_prompt.shape
    bs, ls, _ = x_sample.shape
    depth = norm_mix.shape[0]
    past_len = page_table.shape[1] * cache_sb_k.shape[2]
    pos_p = jnp.arange(lp)
    pos_s = jnp.tile(past_len + jnp.arange(ls), bs)
    lb_all = jnp.cumsum(jax.nn.softmax(hgrn_lb.astype(F32), axis=0), axis=0)
    xp = x_prompt.reshape(bp * lp, d)
    xs = x_sample.reshape(bs * ls, d)
    sbk_p, sbv_p, sbk_s, sbv_s, hg_p, hg_s = [], [], [], [], [], []
    lat_p, rop_p, lat_s, rop_s, cv_p, cv_s = [], [], [], [], [], []
    for layer in range(depth):
        j = layer // 2
        if layer % 2 == 0:
            w_in = w_in_even[j].astype(BF16)
            w_out = w_out_even[j].astype(BF16)
            xp, sp, kp, vp = _even_mixer(xp, (bp, lp), w_in, w_out, lb_all[j], hgrn_gnorm[j], norm_mix[layer])
            xs, ss, ks, vs = _even_mixer(xs, (bs, ls), w_in, w_out, lb_all[j], hgrn_gnorm[j], norm_mix[layer],
                                         s0=state_hgrn[j], caches=(cache_sb_k, cache_sb_v),
                                         page_table=page_table, layer=j)
            sbk_p.append(kp), sbv_p.append(vp), sbk_s.append(ks), sbv_s.append(vs)
            hg_p.append(sp), hg_s.append(ss)
        else:
            split = Q_LORA + KV_LORA
            w_in = jnp.concatenate([w_in_odd[j][:, :split], jnp.tile(w_in_odd[j][:, split:], (1, LANES // QK_ROPE))],
                                   axis=1).astype(BF16)
            wq = w_uq[j].reshape(Q_LORA, H_C, QK_NOPE + QK_ROPE)
            w_uq_perm = jnp.concatenate([wq[:, :, :QK_NOPE].reshape(Q_LORA, -1), wq[:, :, QK_NOPE:].reshape(Q_LORA, -1)],
                                        axis=1).astype(BF16)
            w_out = w_out_odd[j].astype(BF16)
            common = (mla_q_norm[j], mla_kv_norm[j], w_uq_perm, w_uk[j], w_uv[j], w_out)
            xp, cp, rp = _odd_mixer(xp, (bp, lp), pos_p, w_in, norm_mix[layer], *common)
            xs, cs, rs = _odd_mixer(xs, (bs, ls), pos_s, w_in, norm_mix[layer], *common,
                                    caches=(cache_mla_latent, cache_mla_rope), page_table=page_table, layer=j)
            lat_p.append(cp), rop_p.append(rp), lat_s.append(cs), rop_s.append(rs)
        final = norm_final if layer == depth - 1 else None
        ffn_w = (norm_ffn[layer], w_ffn_in[layer].astype(BF16), ffn_conv_w[layer], ffn_conv_b[layer],
                 w_ffn_out[layer].astype(BF16))
        tm_p = _row_tile(lp)
        xp, ap = conv_ffn(xp, *ffn_w, seq_len=lp, tm=tm_p, tn=TN_FF, final_gain=final)
        xs, as_ = conv_ffn(xs, *ffn_w, seq_len=ls, tm=_row_tile(bs * ls), tn=TN_FF, state=state_conv[layer],
                           final_gain=final)
        d_ff = ap.shape[-1]
        cv_p.append(ap.reshape(bp, lp // tm_p, 8, d_ff)[:, -1, 8 - (CONV_W - 1):])
        cv_s.append(as_.reshape(bs, ls, d_ff)[:, ls - (CONV_W - 1):])
    return (xp.reshape(bp, lp, d), xs.reshape(bs, ls, d),
            jnp.stack(sbk_p), jnp.stack(sbv_p), jnp.stack(sbk_s), jnp.stack(sbv_s),
            jnp.stack(hg_p), jnp.stack(hg_s),
            jnp.stack(lat_p), jnp.stack(rop_p), jnp.stack(lat_s), jnp.stack(rop_s),
            jnp.stack(cv_p), jnp.stack(cv_s))
```

```python
import functools
import math

import jax
import jax.numpy as jnp
from jax import lax
from jax.experimental import pallas as pl
from jax.experimental.pallas import tpu as pltpu

F32 = jnp.float32
BF16 = jnp.bfloat16

EPS = 1e-6
ROPE_THETA = 10000.0

H_A, DK_A, DV_A = 4, 128, 128
H_B, DH_B = 8, 64
H_C, QK_NOPE, QK_ROPE, DV_C = 16, 64, 32, 64
KV_LORA, Q_LORA = 256, 384
CONV_W = 3

VMEM_LIMIT_BYTES = 56 * 1024 * 1024
LANES = 128
BF16_ROWS = 16

SB_LOG_ZERO = -104.0


def _cparams(*sem):
    return pltpu.CompilerParams(dimension_semantics=sem, vmem_limit_bytes=VMEM_LIMIT_BYTES)


def _rms(x, g):
    return x * lax.rsqrt(jnp.mean(x * x, axis=-1, keepdims=True) + EPS) * g


def _split3(x):
    hi = x.astype(BF16)
    r = x - hi.astype(F32)
    mid = r.astype(BF16)
    lo = (r - mid.astype(F32)).astype(BF16)
    return hi, mid, lo


def _softplus(z):
    return jnp.maximum(z, 0.0) + jnp.log1p(jnp.exp(-jnp.abs(z)))


def _norm_proj_kernel(x_ref, g_ref, w_ref, *out_refs, outs):
    h = _rms(x_ref[...], g_ref[...]).astype(BF16)
    for o_ref, (off, width, _) in zip(out_refs, outs):
        o_ref[...] = jnp.dot(h, w_ref[:, off:off + width], preferred_element_type=F32).astype(o_ref.dtype)


def norm_proj(x, gain, w, outs, tm):
    m, d = x.shape
    n = w.shape[1]
    return pl.pallas_call(
        functools.partial(_norm_proj_kernel, outs=outs),
        grid=(m // tm,),
        in_specs=[pl.BlockSpec((tm, d), lambda i: (i, 0)),
                  pl.BlockSpec((1, d), lambda i: (0, 0)),
                  pl.BlockSpec((d, n), lambda i: (0, 0))],
        out_specs=[pl.BlockSpec((tm, wd), lambda i: (i, 0)) for _, wd, _ in outs],
        out_shape=[jax.ShapeDtypeStruct((m, wd), dt) for _, wd, dt in outs],
        compiler_params=_cparams("parallel"),
        name="norm_proj",
    )(x, gain.reshape(1, d), w)


def _proj_res_kernel(*refs, n_in):
    a_refs = refs[:n_in]
    w_ref, x_ref, o_ref = refs[n_in:]
    acc = x_ref[...]
    off = 0
    for a_ref in a_refs:
        k = a_ref.shape[1]
        acc = acc + jnp.dot(a_ref[...], w_ref[off:off + k, :], preferred_element_type=F32)
        off += k
    o_ref[...] = acc


def proj_res(a_list, w, x, tm):
    m, d = x.shape
    k = w.shape[0]
    return pl.pallas_call(
        functools.partial(_proj_res_kernel, n_in=len(a_list)),
        grid=(m // tm,),
        in_specs=[pl.BlockSpec((tm, a.shape[1]), lambda i: (i, 0)) for a in a_list]
        + [pl.BlockSpec((k, d), lambda i: (0, 0)), pl.BlockSpec((tm, d), lambda i: (i, 0))],
        out_specs=pl.BlockSpec((tm, d), lambda i: (i, 0)),
        out_shape=jax.ShapeDtypeStruct((m, d), F32),
        compiler_params=_cparams("parallel"),
        name="proj_res",
    )(*a_list, w, x)


def _group_mm_kernel(x_ref, w_ref, o_ref):
    o_ref[...] = jnp.dot(x_ref[...].astype(BF16), w_ref[0], preferred_element_type=F32).astype(o_ref.dtype)


def group_mm(x, w, out_dtype):
    m = x.shape[0]
    p, kin, kout = w.shape
    return pl.pallas_call(
        _group_mm_kernel,
        grid=(p,),
        in_specs=[pl.BlockSpec((m, kin), lambda i: (0, i)), pl.BlockSpec((1, kin, kout), lambda i: (i, 0, 0))],
        out_specs=pl.BlockSpec((m, kout), lambda i: (0, i)),
        out_shape=jax.ShapeDtypeStruct((m, p * kout), out_dtype),
        compiler_params=_cparams("parallel"),
        name="group_mm",
    )(x, w)


def _ffn_kernel(*refs, tm, tc, seq_len, has_state, final_norm, a_rows):
    it = iter(refs)
    x_ref, xp_ref, g_ref, wa_ref, wb_ref, cw_ref, cb_ref, wo_ref = (next(it) for _ in range(8))
    e1_ref = next(it) if has_state else None
    e2_ref = next(it) if has_state else None
    gf_ref = next(it) if final_norm else None
    y_ref, a_ref = next(it), next(it)
    h_scr, acc_scr, a_scr = next(it), next(it), next(it)
    i = pl.program_id(0)
    j = pl.program_id(1)
    pad = BF16_ROWS

    @pl.when(j == 0)
    def _():
        h_scr[0:pad, :] = _rms(xp_ref[...], g_ref[...]).astype(BF16)
        h_scr[pad:, :] = _rms(x_ref[...], g_ref[...]).astype(BF16)
        acc_scr[...] = jnp.zeros_like(acc_scr)

    cw = cw_ref[...]
    for r0 in range(0, tm, tc):
        a_scr[...] = jnp.dot(h_scr[r0:r0 + tc + pad, :], wa_ref[...], preferred_element_type=F32)
        b = jnp.dot(h_scr[pad + r0:pad + r0 + tc, :], wb_ref[...], preferred_element_type=F32)
        a0 = a_scr[pad:, :]
        a1 = a_scr[pad - 1:pad - 1 + tc, :]
        a2 = a_scr[pad - 2:pad - 2 + tc, :]
        t = lax.rem(i * tm + r0 + lax.broadcasted_iota(jnp.int32, (tc, 1), 0), seq_len)
        if has_state:
            s1 = jnp.where(t >= 1, a1, e1_ref[r0:r0 + tc, :])
            s2 = jnp.where(t >= 2, a2, e2_ref[r0:r0 + tc, :])
        else:
            s1 = jnp.where(t >= 1, a1, 0.0)
            s2 = jnp.where(t >= 2, a2, 0.0)
        c = cb_ref[...] + (cw[0:1, :] * s2 + cw[1:2, :] * s1 + cw[2:3, :] * a0)
        gate = (c * jax.nn.sigmoid(c)) * b
        acc_scr[r0:r0 + tc, :] += jnp.dot(gate.astype(BF16), wo_ref[...], preferred_element_type=F32)
        if has_state:
            a_ref[r0:r0 + tc, :] = a0
        elif r0 + tc == tm:
            a_ref[...] = a_scr[pad + tc - a_rows:, :]

    @pl.when(j == pl.num_programs(1) - 1)
    def _():
        y = x_ref[...] + acc_scr[...]
        if final_norm:
            y = _rms(y, gf_ref[...])
        y_ref[...] = y


def conv_ffn(x, gain, w_in, conv_w, conv_b, w_out, *, seq_len, tm, tc, tn, state=None, final_gain=None):
    m, d = x.shape
    d_ff = w_out.shape[0]
    has_state = state is not None
    final_norm = final_gain is not None
    a_rows = tm if has_state else 8
    pad = BF16_ROWS
    nj = d_ff // tn
    tpb = tm // pad
    args = [x, x, gain.reshape(1, d), w_in, w_in, conv_w, conv_b.reshape(1, d_ff), w_out]
    in_specs = [
        pl.BlockSpec((tm, d), lambda i, j: (i, 0)),
        pl.BlockSpec((pad, d), lambda i, j: (jnp.maximum(i * tpb - 1, 0), 0)),
        pl.BlockSpec((1, d), lambda i, j: (0, 0)),
        pl.BlockSpec((d, tn), lambda i, j: (0, j)),
        pl.BlockSpec((d, tn), lambda i, j: (0, nj + j)),
        pl.BlockSpec((CONV_W, tn), lambda i, j: (0, j)),
        pl.BlockSpec((1, tn), lambda i, j: (0, j)),
        pl.BlockSpec((tn, d), lambda i, j: (j, 0)),
    ]
    if has_state:
        b = state.shape[0]
        e1 = jnp.broadcast_to(state[:, 1:2, :], (b, seq_len, d_ff)).reshape(m, d_ff)
        e2 = jnp.tile(state, (1, seq_len // 2, 1)).reshape(m, d_ff)
        args += [e1, e2]
        in_specs += [pl.BlockSpec((tm, tn), lambda i, j: (i, j))] * 2
    if final_norm:
        args.append(final_gain.reshape(1, d))
        in_specs.append(pl.BlockSpec((1, d), lambda i, j: (0, 0)))
    y, a_tail = pl.pallas_call(
        functools.partial(_ffn_kernel, tm=tm, tc=tc, seq_len=seq_len, has_state=has_state,
                          final_norm=final_norm, a_rows=a_rows),
        grid=(m // tm, nj),
        in_specs=in_specs,
        out_specs=[pl.BlockSpec((tm, d), lambda i, j: (i, 0)),
                   pl.BlockSpec((a_rows, tn), lambda i, j: (i, j))],
        out_shape=[jax.ShapeDtypeStruct((m, d), F32),
                   jax.ShapeDtypeStruct((m // tm * a_rows, d_ff), F32)],
        scratch_shapes=[pltpu.VMEM((tm + pad, d), BF16),
                        pltpu.VMEM((tm, d), F32),
                        pltpu.VMEM((tc + pad, tn), F32)],
        compiler_params=_cparams("parallel", "arbitrary"),
        name="conv_ffn",
    )(*args)
    return y, a_tail


def _hgrn_kernel(*refs, chunk, sub, has_s0):
    it = iter(refs)
    q_ref, f_ref, v_ref, g_ref, lb_ref, gn_ref = (next(it) for _ in range(6))
    s0_ref = next(it) if has_s0 else None
    o_ref, s_ref, st_scr = next(it), next(it), next(it)
    c = pl.program_id(1)
    n_sub = chunk // sub

    @pl.when(c == 0)
    def _():
        for h in range(H_A):
            if has_s0:
                st_scr[h] = s0_ref[0, h].T
            else:
                st_scr[h] = jnp.zeros((DV_A, DK_A), F32)

    row = lax.broadcasted_iota(jnp.int32, (chunk, chunk), 0)
    col = lax.broadcasted_iota(jnp.int32, (chunk, chunk), 1)
    tril = jnp.where(col <= row, 1.0, 0.0).astype(BF16)
    tt = lax.broadcasted_iota(jnp.int32, (sub, 1), 0)
    nt_dims = (((1,), (1,)), ((), ()))
    tn_dims = (((0,), (0,)), ((), ()))

    for h in range(H_A):
        sl = slice(h * DK_A, (h + 1) * DK_A)
        q = q_ref[0, :, sl]
        v = v_ref[0, :, sl]
        lb = lb_ref[:, sl]
        f = lb + (1.0 - lb) * jax.nn.sigmoid(f_ref[0, :, sl])
        kk = 1.0 - f
        logf = jnp.log(f)
        bc = sum(jnp.dot(tril, part, preferred_element_type=F32) for part in _split3(logf))
        st = st_scr[h]
        o = lax.dot_general((q * jnp.exp(bc)).astype(BF16), st.astype(BF16), nt_dims,
                            preferred_element_type=F32)
        vb = v.astype(BF16)
        parts = []
        for i in range(n_sub):
            r0 = i * sub
            qi, ki, vi, bi = q[r0:r0 + sub], kk[r0:r0 + sub], v[r0:r0 + sub], bc[r0:r0 + sub]
            oi = o[r0:r0 + sub]
            for s in range(sub):
                p = qi * jnp.exp(bi - bi[s:s + 1]) * ki[s:s + 1]
                att = jnp.where(tt >= s, jnp.sum(p, axis=-1, keepdims=True), 0.0)
                oi = oi + att * vi[s:s + 1]
            if i > 0:
                e = bc[r0 - 1:r0]
                qt = (qi * jnp.exp(bi - e)).astype(BF16)
                kt = (kk[:r0] * jnp.exp(e - bc[:r0])).astype(BF16)
                att = lax.dot_general(qt, kt, nt_dims, preferred_element_type=F32)
                oi = oi + jnp.dot(att.astype(BF16), vb[:r0], preferred_element_type=F32)
            parts.append(oi)
        o = parts[0] if n_sub == 1 else jnp.concatenate(parts, axis=0)
        bl = bc[chunk - 1:chunk]
        kh = (kk * jnp.exp(bl - bc)).astype(BF16)
        st_scr[h] = st * jnp.exp(bl) + lax.dot_general(vb, kh, tn_dims, preferred_element_type=F32)
        on = _rms(o, gn_ref[...])
        ga = g_ref[0, :, sl]
        o_ref[0, :, sl] = (on * (ga * jax.nn.sigmoid(ga))).astype(o_ref.dtype)

    @pl.when(c == pl.num_programs(1) - 1)
    def _():
        for h in range(H_A):
            s_ref[0, h] = st_scr[h].T


def hgrn2(q, f, v, g, lb, gnorm, s0, *, chunk, sub):
    b, l, w = q.shape
    has_s0 = s0 is not None
    seq_spec = pl.BlockSpec((1, chunk, w), lambda i, c: (i, c, 0))
    st_spec = pl.BlockSpec((1, H_A, DK_A, DV_A), lambda i, c: (i, 0, 0, 0))
    args = [q, f, v, g, lb.reshape(1, w), gnorm.reshape(1, DV_A)]
    in_specs = [seq_spec] * 4 + [pl.BlockSpec((1, w), lambda i, c: (0, 0)),
                                 pl.BlockSpec((1, DV_A), lambda i, c: (0, 0))]
    if has_s0:
        args.append(s0)
        in_specs.append(st_spec)
    return pl.pallas_call(
        functools.partial(_hgrn_kernel, chunk=chunk, sub=sub, has_s0=has_s0),
        grid=(b, l // chunk),
        in_specs=in_specs,
        out_specs=[seq_spec, st_spec],
        out_shape=[jax.ShapeDtypeStruct((b, l, w), BF16),
                   jax.ShapeDtypeStruct((b, H_A, DK_A, DV_A), F32)],
        scratch_shapes=[pltpu.VMEM((H_A, DV_A, DK_A), F32)],
        compiler_params=_cparams("parallel", "arbitrary"),
        name="hgrn2",
    )(*args)


def _sb_block(qm, kb, vm, mask, carry, umat, scale, keys_on_lanes=False):
    nt_dims = (((1,), (1,)), ((), ()))
    if keys_on_lanes:
        z = jnp.dot(qm, kb, preferred_element_type=F32) * scale
    else:
        z = lax.dot_general(qm, kb, nt_dims, preferred_element_type=F32) * scale
    lneg = -_softplus(z)
    if mask is not None:
        lneg = jnp.where(mask, lneg, 0.0)
    sb = umat.shape[0]
    n_sub = z.shape[1] // sb
    subs = [lneg[:, i * sb:(i + 1) * sb] for i in range(n_sub)]
    locs = [sum(jnp.dot(part, umat, preferred_element_type=F32) for part in _split3(s)) for s in subs]
    between = [None] * n_sub
    for i in reversed(range(n_sub)):
        between[i] = locs[i] + carry
        carry = carry + locs[i][:, 0:1] + subs[i][:, 0:1]
    between = between[0] if n_sub == 1 else jnp.concatenate(between, axis=1)
    w = jnp.exp(z + lneg + between)
    if mask is not None:
        w = jnp.where(mask, w, 0.0)
    if keys_on_lanes:
        out = lax.dot_general(w.astype(BF16), vm, nt_dims, preferred_element_type=F32)
    else:
        out = jnp.dot(w.astype(BF16), vm, preferred_element_type=F32)
    return out, carry


def _strict_upper(n):
    r = lax.broadcasted_iota(jnp.int32, (n, n), 0)
    c = lax.broadcasted_iota(jnp.int32, (n, n), 1)
    return jnp.where(r > c, 1.0, 0.0).astype(BF16)


def _sb_prompt_kernel(q_ref, k_ref, v_ref, o_ref, *, blk, win):
    qi = pl.program_id(2)
    scale = DH_B ** -0.5
    q = q_ref[0]
    lane = lax.broadcasted_iota(jnp.int32, (1, 2 * DH_B), 1)
    head_lanes = [lane < DH_B, lane >= DH_B]
    qs = [jnp.where(m, q, jnp.zeros_like(q)) for m in head_lanes]
    umat = _strict_upper(blk)
    qpos = qi * blk + lax.broadcasted_iota(jnp.int32, (blk, 1), 0)
    kiota = lax.broadcasted_iota(jnp.int32, (1, win), 1)

    def cond(state):
        hi, c0, c1, _ = state
        live = jnp.maximum(jnp.max(c0), jnp.max(c1)) > SB_LOG_ZERO
        return jnp.logical_and(hi > 0, live)

    def body(state):
        hi, c0, c1, acc = state
        start = pl.multiple_of(jnp.maximum(hi - win, 0), blk)
        kb = k_ref[0, pl.ds(start, win), :]
        vb = v_ref[0, pl.ds(start, win), :]
        mask = (start + kiota) < jnp.minimum(qpos, hi)
        cs = [c0, c1]
        for h in range(2):
            vm = jnp.where(head_lanes[h], vb, jnp.zeros_like(vb))
            out, cs[h] = _sb_block(qs[h], kb, vm, mask, cs[h], umat, scale)
            acc = acc + out
        return start, cs[0], cs[1], acc

    zero_c = jnp.zeros((blk, 1), F32)
    _, _, _, acc = lax.while_loop(cond, body, ((qi + 1) * blk, zero_c, zero_c, jnp.zeros((blk, 2 * DH_B), F32)))
    o_ref[0] = acc.astype(o_ref.dtype)


def sb_prompt(q, k, v, *, blk, win):
    b, l, w = q.shape
    pw = 2 * DH_B
    q_spec = pl.BlockSpec((1, blk, pw), lambda i, p, t: (i, t, p))
    kv_spec = pl.BlockSpec((1, l, pw), lambda i, p, t: (i, 0, p))
    return pl.pallas_call(
        functools.partial(_sb_prompt_kernel, blk=blk, win=win),
        grid=(b, w // pw, l // blk),
        in_specs=[q_spec, kv_spec, kv_spec],
        out_specs=q_spec,
        out_shape=jax.ShapeDtypeStruct((b, l, w), BF16),
        compiler_params=_cparams("parallel", "parallel", "arbitrary"),
        name="sb_prompt",
    )(q, k, v)


def _sb_decode_kernel(pt_ref, q_ref, kn_ref, vn_ref, kc_ref, vc_ref, o_ref, kbuf, vbuf, sem, *, n_new, layer):
    b = pl.program_id(0)
    n_pages = pt_ref.shape[1]
    page = kbuf.shape[2]
    w = q_ref.shape[2]
    rows = q_ref.shape[1]
    scale = DH_B ** -0.5

    def copies(j, slot):
        idx = pt_ref[b, j]
        return (pltpu.make_async_copy(kc_ref.at[layer, idx], kbuf.at[slot], sem.at[0, slot]),
                pltpu.make_async_copy(vc_ref.at[layer, idx], vbuf.at[slot], sem.at[1, slot]))

    for cp in copies(n_pages - 1, 0):
        cp.start()

    lane = lax.broadcasted_iota(jnp.int32, (rows, w), 1)
    rix = lax.broadcasted_iota(jnp.int32, (rows, w), 0)
    own = (lane // DH_B) == (rix // n_new)
    q = q_ref[0]
    qm = jnp.where(own, q, jnp.zeros_like(q))
    umat = _strict_upper(page)

    pad_rows = page - kn_ref.shape[1]
    knew = jnp.concatenate([kn_ref[0], jnp.zeros((pad_rows, w), F32)], axis=0).astype(BF16)
    vnew = jnp.concatenate([vn_ref[0], jnp.zeros((pad_rows, w), F32)], axis=0).astype(BF16)
    tq = lax.rem(lax.broadcasted_iota(jnp.int32, (rows, 1), 0), n_new)
    mask_new = lax.broadcasted_iota(jnp.int32, (1, page), 1) < tq
    acc, carry = _sb_block(qm, knew, vnew, mask_new, jnp.zeros((rows, 1), F32), umat, scale)

    def cond(state):
        j, _, carry, _ = state
        return jnp.logical_and(j >= 0, jnp.max(carry) > SB_LOG_ZERO)

    def body(state):
        j, slot, carry, acc = state

        @pl.when(j > 0)
        def _():
            for cp in copies(j - 1, 1 - slot):
                cp.start()

        for cp in copies(j, slot):
            cp.wait()
        out, carry = _sb_block(qm, kbuf[slot].astype(BF16), vbuf[slot].astype(BF16), None, carry, umat, scale,
                               keys_on_lanes=True)
        return j - 1, 1 - slot, carry, acc + out

    j, slot, _, acc = lax.while_loop(cond, body, (n_pages - 1, 0, carry, acc))

    @pl.when(j >= 0)
    def _():
        for cp in copies(j, slot):
            cp.wait()

    res = jnp.where(own, acc, 0.0)
    o = res[0:n_new]
    for h in range(1, rows // n_new):
        o = o + res[h * n_new:(h + 1) * n_new]
    o_ref[0] = o.astype(o_ref.dtype)


def sb_decode(q_rows, k_new, v_new, k_cache, v_cache, page_table, *, layer, n_new):
    b, rows, w = q_rows.shape
    page = k_cache.shape[3]
    grid_spec = pltpu.PrefetchScalarGridSpec(
        num_scalar_prefetch=1,
        grid=(b,),
        in_specs=[pl.BlockSpec((1, rows, w), lambda i, pt: (i, 0, 0)),
                  pl.BlockSpec((1, k_new.shape[1], w), lambda i, pt: (i, 0, 0)),
                  pl.BlockSpec((1, v_new.shape[1], w), lambda i, pt: (i, 0, 0)),
                  pl.BlockSpec(memory_space=pl.ANY),
                  pl.BlockSpec(memory_space=pl.ANY)],
        out_specs=pl.BlockSpec((1, n_new, w), lambda i, pt: (i, 0, 0)),
        scratch_shapes=[pltpu.VMEM((2, w, page), F32), pltpu.VMEM((2, w, page), F32),
                        pltpu.SemaphoreType.DMA((2, 2))],
    )
    return pl.pallas_call(
        functools.partial(_sb_decode_kernel, n_new=n_new, layer=layer),
        grid_spec=grid_spec,
        out_shape=jax.ShapeDtypeStruct((b, n_new, w), BF16),
        compiler_params=_cparams("arbitrary"),
        name="sb_decode",
    )(page_table, q_rows, k_new, v_new, k_cache, v_cache)


MLA_Q_SCALE = (QK_NOPE + QK_ROPE) ** -0.5 * math.log2(math.e)


def _rope_lanes(x, cos, sin_signed):
    w = x.shape[1]
    lane = lax.broadcasted_iota(jnp.int32, (1, w), 1)
    first = (lane & (QK_ROPE - 1)) < QK_ROPE // 2
    rot = jnp.where(first, pltpu.roll(x, w - QK_ROPE // 2, 1), pltpu.roll(x, QK_ROPE // 2, 1))
    return x * cos + rot * sin_signed


def _mla_q_kernel(cq_ref, g_ref, w_ref, cos_ref, sin_ref, qn_ref, qr_ref):
    n_nope = qn_ref.shape[1]
    h = _rms(cq_ref[...], g_ref[...]).astype(BF16)
    qn = jnp.dot(h, w_ref[:, :n_nope], preferred_element_type=F32)
    qn_ref[...] = (qn * MLA_Q_SCALE).astype(qn_ref.dtype)
    qr = jnp.dot(h, w_ref[:, n_nope:], preferred_element_type=F32)
    reps = qr.shape[1] // LANES
    cos = jnp.concatenate([cos_ref[...]] * reps, axis=1)
    sin = jnp.concatenate([sin_ref[...]] * reps, axis=1)
    qr_ref[...] = (_rope_lanes(qr, cos, sin) * MLA_Q_SCALE).astype(qr_ref.dtype)


def mla_q(cq, gain, w_uq_perm, cos, sin, tm):
    m, ql = cq.shape
    n = w_uq_perm.shape[1]
    n_nope, n_rope = H_C * QK_NOPE, H_C * QK_ROPE
    tab_tiles = cos.shape[0] // tm
    tab_spec = pl.BlockSpec((tm, LANES), lambda i: (i % tab_tiles, 0))
    return pl.pallas_call(
        _mla_q_kernel,
        grid=(m // tm,),
        in_specs=[pl.BlockSpec((tm, ql), lambda i: (i, 0)), pl.BlockSpec((1, ql), lambda i: (0, 0)),
                  pl.BlockSpec((ql, n), lambda i: (0, 0)), tab_spec, tab_spec],
        out_specs=[pl.BlockSpec((tm, n_nope), lambda i: (i, 0)), pl.BlockSpec((tm, n_rope), lambda i: (i, 0))],
        out_shape=[jax.ShapeDtypeStruct((m, n_nope), BF16), jax.ShapeDtypeStruct((m, n_rope), BF16)],
        compiler_params=_cparams("parallel"),
        name="mla_q",
    )(cq, gain.reshape(1, ql), w_uq_perm, cos, sin)


def _mla_kv_kernel(*refs, with_heads):
    ckv_ref, kr_ref, g_ref = refs[:3]
    if with_heads:
        wk_ref, wvt_ref, cos_ref, sin_ref, lat_ref, rope_ref, kn_ref, vt_ref, kr16_ref = refs[3:]
    else:
        cos_ref, sin_ref, lat_ref, rope_ref = refs[3:]
    lat = _rms(ckv_ref[...], g_ref[...])
    lat_ref[...] = lat
    rope = _rope_lanes(kr_ref[...], cos_ref[...], sin_ref[...])
    rope_ref[...] = rope
    if with_heads:
        nt_dims = (((1,), (1,)), ((), ()))
        lb = lat.astype(BF16)
        kn_ref[...] = jnp.dot(lb, wk_ref[...], preferred_element_type=F32).astype(kn_ref.dtype)
        vt_ref[0] = lax.dot_general(wvt_ref[...], lb, nt_dims, preferred_element_type=F32).astype(vt_ref.dtype)
        kr16_ref[...] = rope.astype(kr16_ref.dtype)


def mla_kv(ckv, kr_rep, gain, w_uk, w_uv_t, cos, sin, tm, seq):
    m, c = ckv.shape
    with_heads = w_uk is not None
    tab_tiles = cos.shape[0] // tm
    tab_spec = pl.BlockSpec((tm, LANES), lambda i: (i % tab_tiles, 0))
    row = lambda wd: pl.BlockSpec((tm, wd), lambda i: (i, 0))
    args = [ckv, kr_rep, gain.reshape(1, c)]
    in_specs = [row(c), row(LANES), pl.BlockSpec((1, c), lambda i: (0, 0))]
    out_specs = [row(c), row(LANES)]
    out_shape = [jax.ShapeDtypeStruct((m, c), F32), jax.ShapeDtypeStruct((m, LANES), F32)]
    if with_heads:
        b, l = seq
        n = w_uk.shape[1]
        tiles = l // tm
        args += [w_uk, w_uv_t]
        in_specs += [pl.BlockSpec((c, n), lambda i: (0, 0)), pl.BlockSpec((n, c), lambda i: (0, 0))]
        out_specs += [row(n), pl.BlockSpec((1, n, tm), lambda i: (i // tiles, 0, i % tiles)), row(LANES)]
        out_shape += [jax.ShapeDtypeStruct((m, n), BF16), jax.ShapeDtypeStruct((b, n, l), BF16),
                      jax.ShapeDtypeStruct((m, LANES), BF16)]
    args += [cos, sin]
    in_specs += [tab_spec, tab_spec]
    return pl.pallas_call(
        functools.partial(_mla_kv_kernel, with_heads=with_heads),
        grid=(m // tm,),
        in_specs=in_specs,
        out_specs=out_specs,
        out_shape=out_shape,
        compiler_params=_cparams("parallel"),
        name="mla_kv",
    )(*args)


NEG_BIG = -1e30


def _mla_prompt_kernel(qn_ref, qr_ref, kn_ref, kr_ref, vt_ref, o_ref, *, blk):
    p = pl.program_id(1)
    qi = pl.program_id(2)
    nt_dims = (((1,), (1,)), ((), ()))
    lane = lax.broadcasted_iota(jnp.int32, (1, LANES), 1)
    qn = qn_ref[0]
    qr = qr_ref[0]
    zero = jnp.zeros_like(qn)
    rows = []
    for hh in range(2):
        nope_lanes = (lane >> 6) == hh
        rope_lanes = (lane >> 5) == lax.rem(2 * p + hh, 4)
        rows.append(jnp.concatenate([jnp.where(nope_lanes, qn, zero), jnp.where(rope_lanes, qr, zero)], axis=1))
    qq = jnp.concatenate(rows, axis=0)
    dv = vt_ref.shape[1] // 2

    def logits(j):
        start = pl.multiple_of(j * blk, blk)
        kk = jnp.concatenate([kn_ref[0, pl.ds(start, blk), :], kr_ref[0, pl.ds(start, blk), :]], axis=1)
        return lax.dot_general(kk, qq, nt_dims, preferred_element_type=F32)

    def softmax_pv(j, st, carry, diag):
        m, l, acc0, acc1 = carry
        start = pl.multiple_of(j * blk, blk)
        if diag:
            kpos = lax.broadcasted_iota(jnp.int32, (blk, 2 * blk), 0)
            qpos = lax.broadcasted_iota(jnp.int32, (blk, 2 * blk), 1) & (blk - 1)
            st = jnp.where(kpos <= qpos, st, NEG_BIG)
        m_new = jnp.maximum(m, jnp.max(st, axis=0, keepdims=True))
        alpha = jnp.exp2(m - m_new)
        pr = jnp.exp2(st - m_new)
        l = l * alpha + jnp.sum(pr, axis=0, keepdims=True)
        pb = pr.astype(BF16)
        vt = vt_ref[0, :, pl.ds(start, blk)]
        acc0 = acc0 * alpha[:, :blk] + jnp.dot(vt[:dv], pb[:, :blk], preferred_element_type=F32)
        acc1 = acc1 * alpha[:, blk:] + jnp.dot(vt[dv:], pb[:, blk:], preferred_element_type=F32)
        return m_new, l, acc0, acc1

    init = (jnp.full((1, 2 * blk), NEG_BIG, F32), jnp.zeros((1, 2 * blk), F32),
            jnp.zeros((dv, blk), F32), jnp.zeros((dv, blk), F32))

    def body(j, state):
        st, carry = state
        return logits(j + 1), softmax_pv(j, st, carry, False)

    st, carry = lax.fori_loop(0, qi, body, (logits(0), init))
    _, l, acc0, acc1 = softmax_pv(qi, st, carry, True)
    ot = jnp.concatenate([acc0 / l[:, :blk], acc1 / l[:, blk:]], axis=0)
    o_ref[0] = ot.T.astype(o_ref.dtype)


def mla_prompt(qn, qr, kn, kr16, vt, *, blk):
    b, l, w = qn.shape
    q_spec = pl.BlockSpec((1, blk, LANES), lambda i, p, t: (i, t, p))
    qr_spec = pl.BlockSpec((1, blk, LANES), lambda i, p, t: (i, t, p // 2))
    k_spec = pl.BlockSpec((1, l, LANES), lambda i, p, t: (i, 0, p))
    kr_spec = pl.BlockSpec((1, l, LANES), lambda i, p, t: (i, 0, 0))
    vt_spec = pl.BlockSpec((1, LANES, l), lambda i, p, t: (i, p, 0))
    return pl.pallas_call(
        functools.partial(_mla_prompt_kernel, blk=blk),
        grid=(b, w // LANES, l // blk),
        in_specs=[q_spec, qr_spec, k_spec, kr_spec, vt_spec],
        out_specs=q_spec,
        out_shape=jax.ShapeDtypeStruct((b, l, w), BF16),
        compiler_params=_cparams("parallel", "parallel", "arbitrary"),
        name="mla_prompt",
    )(qn, qr, kn, kr16, vt)


def _mla_decode_kernel(pt_ref, ql_ref, qr_ref, ln_ref, rn_ref, lc_ref, rc_ref, o_ref, lbuf, rbuf, sem,
                       *, group, n_chains, layer):
    b = pl.program_id(0)
    n_pages = pt_ref.shape[1]
    n_groups = n_pages // group
    page = lbuf.shape[2]
    rows = ql_ref.shape[1]
    nt_dims = (((1,), (1,)), ((), ()))

    def copies(bi, g, slot):
        cps = []
        for u in range(group):
            idx = pt_ref[bi, g * group + u]
            cps.append(pltpu.make_async_copy(lc_ref.at[layer, idx], lbuf.at[slot, u], sem.at[0, slot]))
            cps.append(pltpu.make_async_copy(rc_ref.at[layer, idx], rbuf.at[slot, u], sem.at[1, slot]))
        return cps

    @pl.when(b == 0)
    def _():
        for cp in copies(0, 0, 0):
            cp.start()

    ql = ql_ref[0]
    qr = qr_ref[0]

    def attend(lat, s_rope, mask, carry):
        m, l, acc = carry
        latb = lat.astype(BF16)
        s = lax.dot_general(ql, latb, nt_dims, preferred_element_type=F32) + s_rope
        if mask is not None:
            s = jnp.where(mask, s, NEG_BIG)
        m_new = jnp.maximum(m, jnp.max(s, axis=-1, keepdims=True))
        alpha = jnp.exp2(m - m_new)
        pr = jnp.exp2(s - m_new)
        l = l * alpha + jnp.sum(pr, axis=-1, keepdims=True)
        acc = acc * alpha + jnp.dot(pr.astype(BF16), latb, preferred_element_type=F32)
        return m_new, l, acc

    n_pad = ln_ref.shape[1]
    tq = lax.broadcasted_iota(jnp.int32, (rows, n_pad), 0) // H_C
    mask_new = lax.broadcasted_iota(jnp.int32, (rows, n_pad), 1) <= tq
    init = (jnp.full((rows, 1), NEG_BIG, F32), jnp.zeros((rows, 1), F32), jnp.zeros((rows, lbuf.shape[3]), F32))
    s_rope_new = lax.dot_general(qr, rn_ref[0].astype(BF16), nt_dims, preferred_element_type=F32)
    first = attend(ln_ref[0], s_rope_new, mask_new, init)

    per_chain = group // n_chains

    def body(g, carries):
        slot = lax.rem(b * n_groups + g, 2)

        @pl.when(g + 1 < n_groups)
        def _():
            for cp in copies(b, g + 1, 1 - slot):
                cp.start()

        @pl.when(jnp.logical_and(g + 1 == n_groups, b + 1 < pl.num_programs(0)))
        def _():
            for cp in copies(b + 1, 0, 1 - slot):
                cp.start()

        for cp in copies(b, g, slot):
            cp.wait()
        out = []
        for ci in range(n_chains):
            u0 = ci * per_chain
            lat = lbuf[slot, u0:u0 + per_chain].reshape(per_chain * page, lbuf.shape[3])
            s_rope = jnp.concatenate(
                [jnp.dot(qr, rbuf[slot, u].astype(BF16), preferred_element_type=F32)
                 for u in range(u0, u0 + per_chain)], axis=1)
            out.append(attend(lat, s_rope, None, carries[ci]))
        return tuple(out)

    carries = lax.fori_loop(0, n_groups, body, (first,) + (init,) * (n_chains - 1))
    m = carries[0][0]
    for cy in carries[1:]:
        m = jnp.maximum(m, cy[0])
    l = sum(cy[1] * jnp.exp2(cy[0] - m) for cy in carries)
    acc = sum(cy[2] * jnp.exp2(cy[0] - m) for cy in carries)
    o_ref[0] = acc / l


def mla_decode(q_lat, q_rope, lat_new, rope_new, lat_cache, rope_cache, page_table, *, layer, group):
    b, rows, c = q_lat.shape
    r = q_rope.shape[2]
    page = lat_cache.shape[2]
    n_pad = lat_new.shape[1]
    n_chains = 2 if group % 2 == 0 else 1
    grid_spec = pltpu.PrefetchScalarGridSpec(
        num_scalar_prefetch=1,
        grid=(b,),
        in_specs=[pl.BlockSpec((1, rows, c), lambda i, pt: (i, 0, 0)),
                  pl.BlockSpec((1, rows, r), lambda i, pt: (i, 0, 0)),
                  pl.BlockSpec((1, n_pad, c), lambda i, pt: (i, 0, 0)),
                  pl.BlockSpec((1, n_pad, r), lambda i, pt: (i, 0, 0)),
                  pl.BlockSpec(memory_space=pl.ANY),
                  pl.BlockSpec(memory_space=pl.ANY)],
        out_specs=pl.BlockSpec((1, rows, c), lambda i, pt: (i, 0, 0)),
        scratch_shapes=[pltpu.VMEM((2, group, page, c), F32), pltpu.VMEM((2, group, r, page), F32),
                        pltpu.SemaphoreType.DMA((2, 2))],
    )
    return pl.pallas_call(
        functools.partial(_mla_decode_kernel, group=group, n_chains=n_chains, layer=layer),
        grid_spec=grid_spec,
        out_shape=jax.ShapeDtypeStruct((b, rows, c), F32),
        compiler_params=_cparams("arbitrary"),
        name="mla_decode",
    )(page_table, q_lat, q_rope, lat_new, rope_new, lat_cache, rope_cache)


TM = 512
TM_FF = 1024
TC_FF = 512
TN_FF = 1408
HGRN_CHUNK = 64
HGRN_SUB = 16
SB_BLK = 128
SB_WIN = 512
MLA_BLK = 256
MLA_PAGE_GROUP = 16


def _row_tile(m):
    return min(TM, m)


def _rope_tables(pos):
    half = QK_ROPE // 2
    freqs = jnp.exp(-math.log(ROPE_THETA) * jnp.arange(half, dtype=F32) * (2.0 / QK_ROPE))
    ang = pos.astype(F32)[:, None] * freqs[None, :]
    cos, sin = jnp.cos(ang), jnp.sin(ang)
    reps = LANES // QK_ROPE
    return (jnp.tile(jnp.concatenate([cos, cos], axis=1), (1, reps)),
            jnp.tile(jnp.concatenate([-sin, sin], axis=1), (1, reps)))


def _even_mixer(x, seq, w_in, w_out, lb, gnorm, gain, *, s0=None, caches=None, page_table=None, layer=0):
    b, l = seq
    wa, wb = H_A * DK_A, H_B * DH_B
    outs = ((0, wa, F32), (wa, wa, F32), (2 * wa, wa, F32), (3 * wa, wa, F32),
            (4 * wa, wb, BF16), (4 * wa + wb, wb, F32), (4 * wa + 2 * wb, wb, F32))
    if caches is None:
        outs += ((4 * wa + wb, wb, BF16), (4 * wa + 2 * wb, wb, BF16))
    tm = _row_tile(b * l)
    res = norm_proj(x, gain, w_in, outs, tm)
    qa, fa, ia, ga, qb, kb, vb = res[:7]
    r3 = lambda t: t.reshape(b, l, t.shape[-1])
    if caches is None:
        oa, state = hgrn2(r3(qa), r3(fa), r3(ia), r3(ga), lb, gnorm, None, chunk=HGRN_CHUNK, sub=HGRN_SUB)
        ob = sb_prompt(r3(qb), r3(res[7]), r3(res[8]), blk=SB_BLK, win=min(SB_WIN, l))
    else:
        oa, state = hgrn2(r3(qa), r3(fa), r3(ia), r3(ga), lb, gnorm, s0, chunk=l, sub=l)
        k_cache, v_cache = caches
        pad = ((0, 0), (0, 8 - l), (0, 0))
        by_pos = lambda c: jnp.transpose(c, (0, 1, 3, 4, 2)).reshape(c.shape[:2] + (wb, c.shape[2]))
        ob = sb_decode(jnp.tile(r3(qb), (1, H_B, 1)), jnp.pad(r3(kb), pad), jnp.pad(r3(vb), pad),
                       by_pos(k_cache), by_pos(v_cache), page_table, layer=layer, n_new=l)
    x = proj_res([oa.reshape(b * l, wa), ob.reshape(b * l, wb)], w_out, x, tm)
    return x, state, kb.reshape(b, l, H_B, DH_B), vb.reshape(b, l, H_B, DH_B)


def _odd_mixer(x, seq, pos, w_in_ext, gain, qn_g, kvn_g, w_uq_perm, w_uk, w_uv, w_out,
               *, caches=None, page_table=None, layer=0):
    b, l = seq
    m = b * l
    tm = _row_tile(m)
    cq, ckv, kr_rep = norm_proj(x, gain, w_in_ext,
                                ((0, Q_LORA, F32), (Q_LORA, KV_LORA, F32), (Q_LORA + KV_LORA, LANES, F32)), tm)
    cos, sin = _rope_tables(pos)
    qn, qr = mla_q(cq, qn_g, w_uq_perm, cos, sin, tm)
    if caches is None:
        w_uk2 = w_uk.reshape(KV_LORA, -1).astype(BF16)
        w_uv_t = w_uv.reshape(KV_LORA, -1).T.astype(BF16)
        lat, rope, kn, vt, kr16 = mla_kv(ckv, kr_rep, kvn_g, w_uk2, w_uv_t, cos, sin, tm, seq)
        r3 = lambda t: t.reshape(b, l, t.shape[-1])
        o = mla_prompt(r3(qn), r3(qr), r3(kn), r3(kr16), vt, blk=min(MLA_BLK, l)).reshape(m, -1)
    else:
        lat, rope = mla_kv(ckv, kr_rep, kvn_g, None, None, cos, sin, tm, seq)
        eye2 = jnp.eye(2, dtype=F32)
        wk = jnp.transpose(w_uk, (1, 2, 0)).reshape(H_C // 2, 2, QK_NOPE, KV_LORA)
        wk = jnp.einsum('pand,ab->panbd', wk, eye2).reshape(H_C // 2, 2 * QK_NOPE, 2 * KV_LORA).astype(BF16)
        wv = jnp.transpose(w_uv, (1, 0, 2)).reshape(H_C // 2, 2, KV_LORA, DV_C)
        wv = jnp.einsum('pacv,ab->pacbv', wv, eye2).reshape(H_C // 2, 2 * KV_LORA, 2 * DV_C).astype(BF16)
        q_lat = group_mm(qn, wk, BF16).reshape(b, l * H_C, KV_LORA)
        lat_cache, rope_cache = caches
        pad = ((0, 0), (0, 8 - l), (0, 0))
        o_lat = mla_decode(q_lat, qr.reshape(b, l * H_C, QK_ROPE),
                           jnp.pad(lat.reshape(b, l, KV_LORA), pad),
                           jnp.pad(rope[:, :QK_ROPE].reshape(b, l, QK_ROPE), pad),
                           lat_cache, jnp.transpose(rope_cache, (0, 1, 3, 2)), page_table, layer=layer,
                           group=math.gcd(MLA_PAGE_GROUP, page_table.shape[1]))
        o = group_mm(o_lat.reshape(m, H_C * KV_LORA), wv, BF16)
    x = proj_res([o], w_out, x, tm)
    return x, lat.reshape(b, l, KV_LORA), rope[:, :QK_ROPE].reshape(b, l, QK_ROPE)


def kernel(x_prompt, x_sample, cache_sb_k, cache_sb_v, cache_mla_latent, cache_mla_rope, state_hgrn, state_conv, page_table, w_in_even, w_out_even, hgrn_lb, hgrn_gnorm, w_in_odd, mla_q_norm, mla_kv_norm, w_uq, w_uk, w_uv, w_out_odd, norm_mix, norm_ffn, w_ffn_in, ffn_conv_w, ffn_conv_b, w_ffn_out, norm_final):
    bp, lp, d = x_prompt.shape
    bs, ls, _ = x_sample.shape
    depth = norm_mix.shape[0]
    past_len = page_table.shape[1] * cache_sb_k.shape[2]
    pos_p = jnp.arange(lp)
    pos_s = jnp.tile(past_len + jnp.arange(ls), bs)
    lb_all = jnp.cumsum(jax.nn.softmax(hgrn_lb.astype(F32), axis=0), axis=0)
    xp = x_prompt.reshape(bp * lp, d)
    xs = x_sample.reshape(bs * ls, d)
    sbk_p, sbv_p, sbk_s, sbv_s, hg_p, hg_s = [], [], [], [], [], []
    lat_p, rop_p, lat_s, rop_s, cv_p, cv_s = [], [], [], [], [], []
    for layer in range(depth):
        j = layer // 2
        if layer % 2 == 0:
            w_in = w_in_even[j].astype(BF16)
            w_out = w_out_even[j].astype(BF16)
            xp, sp, kp, vp = _even_mixer(xp, (bp, lp), w_in, w_out, lb_all[j], hgrn_gnorm[j], norm_mix[layer])
            xs, ss, ks, vs = _even_mixer(xs, (bs, ls), w_in, w_out, lb_all[j], hgrn_gnorm[j], norm_mix[layer],
                                         s0=state_hgrn[j], caches=(cache_sb_k, cache_sb_v),
                                         page_table=page_table, layer=j)
            sbk_p.append(kp), sbv_p.append(vp), sbk_s.append(ks), sbv_s.append(vs)
            hg_p.append(sp), hg_s.append(ss)
        else:
            split = Q_LORA + KV_LORA
            w_in = jnp.concatenate([w_in_odd[j][:, :split], jnp.tile(w_in_odd[j][:, split:], (1, LANES // QK_ROPE))],
                                   axis=1).astype(BF16)
            wq = w_uq[j].reshape(Q_LORA, H_C, QK_NOPE + QK_ROPE)
            w_uq_perm = jnp.concatenate([wq[:, :, :QK_NOPE].reshape(Q_LORA, -1), wq[:, :, QK_NOPE:].reshape(Q_LORA, -1)],
                                        axis=1).astype(BF16)
            w_out = w_out_odd[j].astype(BF16)
            common = (mla_q_norm[j], mla_kv_norm[j], w_uq_perm, w_uk[j], w_uv[j], w_out)
            xp, cp, rp = _odd_mixer(xp, (bp, lp), pos_p, w_in, norm_mix[layer], *common)
            xs, cs, rs = _odd_mixer(xs, (bs, ls), pos_s, w_in, norm_mix[layer], *common,
                                    caches=(cache_mla_latent, cache_mla_rope), page_table=page_table, layer=j)
            lat_p.append(cp), rop_p.append(rp), lat_s.append(cs), rop_s.append(rs)
        final = norm_final if layer == depth - 1 else None
        ffn_w = (norm_ffn[layer], w_ffn_in[layer].astype(BF16), ffn_conv_w[layer], ffn_conv_b[layer],
                 w_ffn_out[layer].astype(BF16))
        tm_p = min(TM_FF, lp)
        tm_s = _row_tile(bs * ls)
        xp, ap = conv_ffn(xp, *ffn_w, seq_len=lp, tm=tm_p, tc=min(TC_FF, tm_p), tn=TN_FF, final_gain=final)
        xs, as_ = conv_ffn(xs, *ffn_w, seq_len=ls, tm=tm_s, tc=min(TC_FF, tm_s), tn=TN_FF,
                           state=state_conv[layer], final_gain=final)
        d_ff = ap.shape[-1]
        cv_p.append(ap.reshape(bp, lp // tm_p, 8, d_ff)[:, -1, 8 - (CONV_W - 1):])
        cv_s.append(as_.reshape(bs, ls, d_ff)[:, ls - (CONV_W - 1):])
    return (xp.reshape(bp, lp, d), xs.reshape(bs, ls, d),
            jnp.stack(sbk_p), jnp.stack(sbv_p), jnp.stack(sbk_s), jnp.stack(sbv_s),
            jnp.stack(hg_p), jnp.stack(hg_s),
            jnp.stack(lat_p), jnp.stack(rop_p), jnp.stack(lat_s), jnp.stack(rop_s),
            jnp.stack(cv_p), jnp.stack(cv_s))
```

```python
import functools
import math

import jax
import jax.numpy as jnp
from jax import lax
from jax.experimental import pallas as pl
from jax.experimental.pallas import tpu as pltpu

F32 = jnp.float32
BF16 = jnp.bfloat16

EPS = 1e-6
ROPE_THETA = 10000.0

H_A, DK_A, DV_A = 4, 128, 128
H_B, DH_B = 8, 64
H_C, QK_NOPE, QK_ROPE, DV_C = 16, 64, 32, 64
KV_LORA, Q_LORA = 256, 384
CONV_W = 3

VMEM_LIMIT_BYTES = 56 * 1024 * 1024
LANES = 128
BF16_ROWS = 16

SB_LOG_ZERO = -104.0


def _cparams(*sem):
    return pltpu.CompilerParams(dimension_semantics=sem, vmem_limit_bytes=VMEM_LIMIT_BYTES)


def _rms(x, g):
    return x * lax.rsqrt(jnp.mean(x * x, axis=-1, keepdims=True) + EPS) * g


def _split3(x):
    hi = x.astype(BF16)
    r = x - hi.astype(F32)
    mid = r.astype(BF16)
    lo = (r - mid.astype(F32)).astype(BF16)
    return hi, mid, lo


def _softplus(z):
    return jnp.maximum(z, 0.0) + jnp.log1p(jnp.exp(-jnp.abs(z)))


def _norm_proj_kernel(x_ref, g_ref, w_ref, *out_refs, outs):
    h = _rms(x_ref[...], g_ref[...]).astype(BF16)
    for o_ref, (off, width, _) in zip(out_refs, outs):
        o_ref[...] = jnp.dot(h, w_ref[:, off:off + width], preferred_element_type=F32).astype(o_ref.dtype)


def norm_proj(x, gain, w, outs, tm):
    m, d = x.shape
    n = w.shape[1]
    return pl.pallas_call(
        functools.partial(_norm_proj_kernel, outs=outs),
        grid=(m // tm,),
        in_specs=[pl.BlockSpec((tm, d), lambda i: (i, 0)),
                  pl.BlockSpec((1, d), lambda i: (0, 0)),
                  pl.BlockSpec((d, n), lambda i: (0, 0))],
        out_specs=[pl.BlockSpec((tm, wd), lambda i: (i, 0)) for _, wd, _ in outs],
        out_shape=[jax.ShapeDtypeStruct((m, wd), dt) for _, wd, dt in outs],
        compiler_params=_cparams("parallel"),
        name="norm_proj",
    )(x, gain.reshape(1, d), w)


def _proj_res_kernel(*refs, n_in):
    a_refs = refs[:n_in]
    w_ref, x_ref, o_ref = refs[n_in:]
    acc = x_ref[...]
    off = 0
    for a_ref in a_refs:
        k = a_ref.shape[1]
        acc = acc + jnp.dot(a_ref[...], w_ref[off:off + k, :], preferred_element_type=F32)
        off += k
    o_ref[...] = acc


def proj_res(a_list, w, x, tm):
    m, d = x.shape
    k = w.shape[0]
    return pl.pallas_call(
        functools.partial(_proj_res_kernel, n_in=len(a_list)),
        grid=(m // tm,),
        in_specs=[pl.BlockSpec((tm, a.shape[1]), lambda i: (i, 0)) for a in a_list]
        + [pl.BlockSpec((k, d), lambda i: (0, 0)), pl.BlockSpec((tm, d), lambda i: (i, 0))],
        out_specs=pl.BlockSpec((tm, d), lambda i: (i, 0)),
        out_shape=jax.ShapeDtypeStruct((m, d), F32),
        compiler_params=_cparams("parallel"),
        name="proj_res",
    )(*a_list, w, x)


def _group_mm_kernel(x_ref, w_ref, o_ref):
    o_ref[...] = jnp.dot(x_ref[...].astype(BF16), w_ref[0], preferred_element_type=F32).astype(o_ref.dtype)


def group_mm(x, w, out_dtype):
    m = x.shape[0]
    p, kin, kout = w.shape
    return pl.pallas_call(
        _group_mm_kernel,
        grid=(p,),
        in_specs=[pl.BlockSpec((m, kin), lambda i: (0, i)), pl.BlockSpec((1, kin, kout), lambda i: (i, 0, 0))],
        out_specs=pl.BlockSpec((m, kout), lambda i: (0, i)),
        out_shape=jax.ShapeDtypeStruct((m, p * kout), out_dtype),
        compiler_params=_cparams("parallel"),
        name="group_mm",
    )(x, w)


def _ffn_kernel(*refs, tm, tc, seq_len, has_state, final_norm, a_rows):
    it = iter(refs)
    x_ref, xp_ref, g_ref, wa_ref, wb_ref, cw_ref, cb_ref, wo_ref = (next(it) for _ in range(8))
    e1_ref = next(it) if has_state else None
    e2_ref = next(it) if has_state else None
    gf_ref = next(it) if final_norm else None
    y_ref, a_ref = next(it), next(it)
    h_scr, acc_scr, a_scr = next(it), next(it), next(it)
    i = pl.program_id(0)
    j = pl.program_id(1)
    pad = BF16_ROWS

    @pl.when(j == 0)
    def _():
        h_scr[0:pad, :] = _rms(xp_ref[...], g_ref[...]).astype(BF16)
        h_scr[pad:, :] = _rms(x_ref[...], g_ref[...]).astype(BF16)
        acc_scr[...] = jnp.zeros_like(acc_scr)

    cw = cw_ref[...]
    for r0 in range(0, tm, tc):
        a_scr[...] = jnp.dot(h_scr[r0:r0 + tc + pad, :], wa_ref[...], preferred_element_type=F32)
        b = jnp.dot(h_scr[pad + r0:pad + r0 + tc, :], wb_ref[...], preferred_element_type=F32)
        a0 = a_scr[pad:, :]
        a1 = a_scr[pad - 1:pad - 1 + tc, :]
        a2 = a_scr[pad - 2:pad - 2 + tc, :]
        t = lax.rem(i * tm + r0 + lax.broadcasted_iota(jnp.int32, (tc, 1), 0), seq_len)
        if has_state:
            s1 = jnp.where(t >= 1, a1, e1_ref[r0:r0 + tc, :])
            s2 = jnp.where(t >= 2, a2, e2_ref[r0:r0 + tc, :])
        else:
            s1 = jnp.where(t >= 1, a1, 0.0)
            s2 = jnp.where(t >= 2, a2, 0.0)
        c = cb_ref[...] + (cw[0:1, :] * s2 + cw[1:2, :] * s1 + cw[2:3, :] * a0)
        gate = (c * jax.nn.sigmoid(c)) * b
        acc_scr[r0:r0 + tc, :] += jnp.dot(gate.astype(BF16), wo_ref[...], preferred_element_type=F32)
        if has_state:
            a_ref[r0:r0 + tc, :] = a0
        elif r0 + tc == tm:
            a_ref[...] = a_scr[pad + tc - a_rows:, :]

    @pl.when(j == pl.num_programs(1) - 1)
    def _():
        y = x_ref[...] + acc_scr[...]
        if final_norm:
            y = _rms(y, gf_ref[...])
        y_ref[...] = y


def conv_ffn(x, gain, w_in, conv_w, conv_b, w_out, *, seq_len, tm, tc, tn, state=None, final_gain=None):
    m, d = x.shape
    d_ff = w_out.shape[0]
    has_state = state is not None
    final_norm = final_gain is not None
    a_rows = tm if has_state else 8
    pad = BF16_ROWS
    nj = d_ff // tn
    tpb = tm // pad
    args = [x, x, gain.reshape(1, d), w_in, w_in, conv_w, conv_b.reshape(1, d_ff), w_out]
    in_specs = [
        pl.BlockSpec((tm, d), lambda i, j: (i, 0)),
        pl.BlockSpec((pad, d), lambda i, j: (jnp.maximum(i * tpb - 1, 0), 0)),
        pl.BlockSpec((1, d), lambda i, j: (0, 0)),
        pl.BlockSpec((d, tn), lambda i, j: (0, j)),
        pl.BlockSpec((d, tn), lambda i, j: (0, nj + j)),
        pl.BlockSpec((CONV_W, tn), lambda i, j: (0, j)),
        pl.BlockSpec((1, tn), lambda i, j: (0, j)),
        pl.BlockSpec((tn, d), lambda i, j: (j, 0)),
    ]
    if has_state:
        b = state.shape[0]
        e1 = jnp.broadcast_to(state[:, 1:2, :], (b, seq_len, d_ff)).reshape(m, d_ff)
        e2 = jnp.tile(state, (1, seq_len // 2, 1)).reshape(m, d_ff)
        args += [e1, e2]
        in_specs += [pl.BlockSpec((tm, tn), lambda i, j: (i, j))] * 2
    if final_norm:
        args.append(final_gain.reshape(1, d))
        in_specs.append(pl.BlockSpec((1, d), lambda i, j: (0, 0)))
    y, a_tail = pl.pallas_call(
        functools.partial(_ffn_kernel, tm=tm, tc=tc, seq_len=seq_len, has_state=has_state,
                          final_norm=final_norm, a_rows=a_rows),
        grid=(m // tm, nj),
        in_specs=in_specs,
        out_specs=[pl.BlockSpec((tm, d), lambda i, j: (i, 0)),
                   pl.BlockSpec((a_rows, tn), lambda i, j: (i, j))],
        out_shape=[jax.ShapeDtypeStruct((m, d), F32),
                   jax.ShapeDtypeStruct((m // tm * a_rows, d_ff), F32)],
        scratch_shapes=[pltpu.VMEM((tm + pad, d), BF16),
                        pltpu.VMEM((tm, d), F32),
                        pltpu.VMEM((tc + pad, tn), F32)],
        compiler_params=_cparams("parallel", "arbitrary"),
        name="conv_ffn",
    )(*args)
    return y, a_tail


def _hgrn_kernel(*refs, chunk, sub, has_s0):
    it = iter(refs)
    q_ref, f_ref, v_ref, g_ref, lb_ref, gn_ref = (next(it) for _ in range(6))
    s0_ref = next(it) if has_s0 else None
    o_ref, s_ref, st_scr = next(it), next(it), next(it)
    c = pl.program_id(1)
    n_sub = chunk // sub

    @pl.when(c == 0)
    def _():
        for h in range(H_A):
            if has_s0:
                st_scr[h] = s0_ref[0, h].T
            else:
                st_scr[h] = jnp.zeros((DV_A, DK_A), F32)

    row = lax.broadcasted_iota(jnp.int32, (chunk, chunk), 0)
    col = lax.broadcasted_iota(jnp.int32, (chunk, chunk), 1)
    tril = jnp.where(col <= row, 1.0, 0.0).astype(BF16)
    tt = lax.broadcasted_iota(jnp.int32, (sub, 1), 0)
    nt_dims = (((1,), (1,)), ((), ()))
    tn_dims = (((0,), (0,)), ((), ()))

    for h in range(H_A):
        sl = slice(h * DK_A, (h + 1) * DK_A)
        q = q_ref[0, :, sl]
        v = v_ref[0, :, sl]
        lb = lb_ref[:, sl]
        f = lb + (1.0 - lb) * jax.nn.sigmoid(f_ref[0, :, sl])
        kk = 1.0 - f
        logf = jnp.log(f)
        bc = sum(jnp.dot(tril, part, preferred_element_type=F32) for part in _split3(logf))
        st = st_scr[h]
        o = lax.dot_general((q * jnp.exp(bc)).astype(BF16), st.astype(BF16), nt_dims,
                            preferred_element_type=F32)
        vb = v.astype(BF16)
        parts = []
        for i in range(n_sub):
            r0 = i * sub
            qi, ki, vi, bi = q[r0:r0 + sub], kk[r0:r0 + sub], v[r0:r0 + sub], bc[r0:r0 + sub]
            oi = o[r0:r0 + sub]
            for s in range(sub):
                p = qi * jnp.exp(bi - bi[s:s + 1]) * ki[s:s + 1]
                att = jnp.where(tt >= s, jnp.sum(p, axis=-1, keepdims=True), 0.0)
                oi = oi + att * vi[s:s + 1]
            if i > 0:
                e = bc[r0 - 1:r0]
                qt = (qi * jnp.exp(bi - e)).astype(BF16)
                kt = (kk[:r0] * jnp.exp(e - bc[:r0])).astype(BF16)
                att = lax.dot_general(qt, kt, nt_dims, preferred_element_type=F32)
                oi = oi + jnp.dot(att.astype(BF16), vb[:r0], preferred_element_type=F32)
            parts.append(oi)
        o = parts[0] if n_sub == 1 else jnp.concatenate(parts, axis=0)
        bl = bc[chunk - 1:chunk]
        kh = (kk * jnp.exp(bl - bc)).astype(BF16)
        st_scr[h] = st * jnp.exp(bl) + lax.dot_general(vb, kh, tn_dims, preferred_element_type=F32)
        on = _rms(o, gn_ref[...])
        ga = g_ref[0, :, sl]
        o_ref[0, :, sl] = (on * (ga * jax.nn.sigmoid(ga))).astype(o_ref.dtype)

    @pl.when(c == pl.num_programs(1) - 1)
    def _():
        for h in range(H_A):
            s_ref[0, h] = st_scr[h].T


def hgrn2(q, f, v, g, lb, gnorm, s0, *, chunk, sub):
    b, l, w = q.shape
    has_s0 = s0 is not None
    seq_spec = pl.BlockSpec((1, chunk, w), lambda i, c: (i, c, 0))
    st_spec = pl.BlockSpec((1, H_A, DK_A, DV_A), lambda i, c: (i, 0, 0, 0))
    args = [q, f, v, g, lb.reshape(1, w), gnorm.reshape(1, DV_A)]
    in_specs = [seq_spec] * 4 + [pl.BlockSpec((1, w), lambda i, c: (0, 0)),
                                 pl.BlockSpec((1, DV_A), lambda i, c: (0, 0))]
    if has_s0:
        args.append(s0)
        in_specs.append(st_spec)
    return pl.pallas_call(
        functools.partial(_hgrn_kernel, chunk=chunk, sub=sub, has_s0=has_s0),
        grid=(b, l // chunk),
        in_specs=in_specs,
        out_specs=[seq_spec, st_spec],
        out_shape=[jax.ShapeDtypeStruct((b, l, w), BF16),
                   jax.ShapeDtypeStruct((b, H_A, DK_A, DV_A), F32)],
        scratch_shapes=[pltpu.VMEM((H_A, DV_A, DK_A), F32)],
        compiler_params=_cparams("parallel", "arbitrary"),
        name="hgrn2",
    )(*args)


def _sb_block(qm, kb, vm, mask, carry, umat, scale, keys_on_lanes=False):
    nt_dims = (((1,), (1,)), ((), ()))
    if keys_on_lanes:
        z = jnp.dot(qm, kb, preferred_element_type=F32) * scale
    else:
        z = lax.dot_general(qm, kb, nt_dims, preferred_element_type=F32) * scale
    lneg = -_softplus(z)
    if mask is not None:
        lneg = jnp.where(mask, lneg, 0.0)
    sb = umat.shape[0]
    n_sub = z.shape[1] // sb
    subs = [lneg[:, i * sb:(i + 1) * sb] for i in range(n_sub)]
    locs = [sum(jnp.dot(part, umat, preferred_element_type=F32) for part in _split3(s)) for s in subs]
    between = [None] * n_sub
    for i in reversed(range(n_sub)):
        between[i] = locs[i] + carry
        carry = carry + locs[i][:, 0:1] + subs[i][:, 0:1]
    between = between[0] if n_sub == 1 else jnp.concatenate(between, axis=1)
    w = jnp.exp(z + lneg + between)
    if mask is not None:
        w = jnp.where(mask, w, 0.0)
    if keys_on_lanes:
        out = lax.dot_general(w.astype(BF16), vm, nt_dims, preferred_element_type=F32)
    else:
        out = jnp.dot(w.astype(BF16), vm, preferred_element_type=F32)
    return out, carry


def _strict_upper(n):
    r = lax.broadcasted_iota(jnp.int32, (n, n), 0)
    c = lax.broadcasted_iota(jnp.int32, (n, n), 1)
    return jnp.where(r > c, 1.0, 0.0).astype(BF16)


def _sb_prompt_kernel(q_ref, k_ref, v_ref, o_ref, *, blk, win):
    qi = pl.program_id(2)
    scale = DH_B ** -0.5
    q = q_ref[0]
    lane = lax.broadcasted_iota(jnp.int32, (1, 2 * DH_B), 1)
    head_lanes = [lane < DH_B, lane >= DH_B]
    qs = [jnp.where(m, q, jnp.zeros_like(q)) for m in head_lanes]
    umat = _strict_upper(blk)
    qpos = qi * blk + lax.broadcasted_iota(jnp.int32, (blk, 1), 0)
    kiota = lax.broadcasted_iota(jnp.int32, (1, win), 1)

    def cond(state):
        hi, c0, c1, _ = state
        live = jnp.maximum(jnp.max(c0), jnp.max(c1)) > SB_LOG_ZERO
        return jnp.logical_and(hi > 0, live)

    def body(state):
        hi, c0, c1, acc = state
        start = pl.multiple_of(jnp.maximum(hi - win, 0), blk)
        kb = k_ref[0, pl.ds(start, win), :]
        vb = v_ref[0, pl.ds(start, win), :]
        mask = (start + kiota) < jnp.minimum(qpos, hi)
        cs = [c0, c1]
        for h in range(2):
            vm = jnp.where(head_lanes[h], vb, jnp.zeros_like(vb))
            out, cs[h] = _sb_block(qs[h], kb, vm, mask, cs[h], umat, scale)
            acc = acc + out
        return start, cs[0], cs[1], acc

    zero_c = jnp.zeros((blk, 1), F32)
    _, _, _, acc = lax.while_loop(cond, body, ((qi + 1) * blk, zero_c, zero_c, jnp.zeros((blk, 2 * DH_B), F32)))
    o_ref[0] = acc.astype(o_ref.dtype)


def sb_prompt(q, k, v, *, blk, win):
    b, l, w = q.shape
    pw = 2 * DH_B
    q_spec = pl.BlockSpec((1, blk, pw), lambda i, p, t: (i, t, p))
    kv_spec = pl.BlockSpec((1, l, pw), lambda i, p, t: (i, 0, p))
    return pl.pallas_call(
        functools.partial(_sb_prompt_kernel, blk=blk, win=win),
        grid=(b, w // pw, l // blk),
        in_specs=[q_spec, kv_spec, kv_spec],
        out_specs=q_spec,
        out_shape=jax.ShapeDtypeStruct((b, l, w), BF16),
        compiler_params=_cparams("parallel", "parallel", "arbitrary"),
        name="sb_prompt",
    )(q, k, v)


def _sb_decode_kernel(pt_ref, q_ref, kn_ref, vn_ref, kc_ref, vc_ref, o_ref, kbuf, vbuf, sem, *, n_new, layer):
    b = pl.program_id(0)
    n_pages = pt_ref.shape[1]
    page = kbuf.shape[2]
    w = q_ref.shape[2]
    rows = q_ref.shape[1]
    scale = DH_B ** -0.5

    def copies(j, slot):
        idx = pt_ref[b, j]
        return (pltpu.make_async_copy(kc_ref.at[layer, idx], kbuf.at[slot], sem.at[0, slot]),
                pltpu.make_async_copy(vc_ref.at[layer, idx], vbuf.at[slot], sem.at[1, slot]))

    for cp in copies(n_pages - 1, 0):
        cp.start()

    lane = lax.broadcasted_iota(jnp.int32, (rows, w), 1)
    rix = lax.broadcasted_iota(jnp.int32, (rows, w), 0)
    own = (lane // DH_B) == (rix // n_new)
    q = q_ref[0]
    qm = jnp.where(own, q, jnp.zeros_like(q))
    umat = _strict_upper(page)

    pad_rows = page - kn_ref.shape[1]
    knew = jnp.concatenate([kn_ref[0], jnp.zeros((pad_rows, w), F32)], axis=0).astype(BF16)
    vnew = jnp.concatenate([vn_ref[0], jnp.zeros((pad_rows, w), F32)], axis=0).astype(BF16)
    tq = lax.rem(lax.broadcasted_iota(jnp.int32, (rows, 1), 0), n_new)
    mask_new = lax.broadcasted_iota(jnp.int32, (1, page), 1) < tq
    acc, carry = _sb_block(qm, knew, vnew, mask_new, jnp.zeros((rows, 1), F32), umat, scale)

    def cond(state):
        j, _, carry, _ = state
        return jnp.logical_and(j >= 0, jnp.max(carry) > SB_LOG_ZERO)

    def body(state):
        j, slot, carry, acc = state

        @pl.when(j > 0)
        def _():
            for cp in copies(j - 1, 1 - slot):
                cp.start()

        for cp in copies(j, slot):
            cp.wait()
        out, carry = _sb_block(qm, kbuf[slot].astype(BF16), vbuf[slot].astype(BF16), None, carry, umat, scale,
                               keys_on_lanes=True)
        return j - 1, 1 - slot, carry, acc + out

    j, slot, _, acc = lax.while_loop(cond, body, (n_pages - 1, 0, carry, acc))

    @pl.when(j >= 0)
    def _():
        for cp in copies(j, slot):
            cp.wait()

    res = jnp.where(own, acc, 0.0)
    o = res[0:n_new]
    for h in range(1, rows // n_new):
        o = o + res[h * n_new:(h + 1) * n_new]
    o_ref[0] = o.astype(o_ref.dtype)


def sb_decode(q_rows, k_new, v_new, k_cache, v_cache, page_table, *, layer, n_new):
    b, rows, w = q_rows.shape
    page = k_cache.shape[3]
    grid_spec = pltpu.PrefetchScalarGridSpec(
        num_scalar_prefetch=1,
        grid=(b,),
        in_specs=[pl.BlockSpec((1, rows, w), lambda i, pt: (i, 0, 0)),
                  pl.BlockSpec((1, k_new.shape[1], w), lambda i, pt: (i, 0, 0)),
                  pl.BlockSpec((1, v_new.shape[1], w), lambda i, pt: (i, 0, 0)),
                  pl.BlockSpec(memory_space=pl.ANY),
                  pl.BlockSpec(memory_space=pl.ANY)],
        out_specs=pl.BlockSpec((1, n_new, w), lambda i, pt: (i, 0, 0)),
        scratch_shapes=[pltpu.VMEM((2, w, page), F32), pltpu.VMEM((2, w, page), F32),
                        pltpu.SemaphoreType.DMA((2, 2))],
    )
    return pl.pallas_call(
        functools.partial(_sb_decode_kernel, n_new=n_new, layer=layer),
        grid_spec=grid_spec,
        out_shape=jax.ShapeDtypeStruct((b, n_new, w), BF16),
        compiler_params=_cparams("arbitrary"),
        name="sb_decode",
    )(page_table, q_rows, k_new, v_new, k_cache, v_cache)


MLA_Q_SCALE = (QK_NOPE + QK_ROPE) ** -0.5 * math.log2(math.e)


def _rope_lanes(x, cos, sin_signed):
    w = x.shape[1]
    lane = lax.broadcasted_iota(jnp.int32, (1, w), 1)
    first = (lane & (QK_ROPE - 1)) < QK_ROPE // 2
    rot = jnp.where(first, pltpu.roll(x, w - QK_ROPE // 2, 1), pltpu.roll(x, QK_ROPE // 2, 1))
    return x * cos + rot * sin_signed


def _mla_q_kernel(cq_ref, g_ref, w_ref, cos_ref, sin_ref, qn_ref, qr_ref):
    n_nope = qn_ref.shape[1]
    h = _rms(cq_ref[...], g_ref[...]).astype(BF16)
    qn = jnp.dot(h, w_ref[:, :n_nope], preferred_element_type=F32)
    qn_ref[...] = (qn * MLA_Q_SCALE).astype(qn_ref.dtype)
    qr = jnp.dot(h, w_ref[:, n_nope:], preferred_element_type=F32)
    reps = qr.shape[1] // LANES
    cos = jnp.concatenate([cos_ref[...]] * reps, axis=1)
    sin = jnp.concatenate([sin_ref[...]] * reps, axis=1)
    qr_ref[...] = (_rope_lanes(qr, cos, sin) * MLA_Q_SCALE).astype(qr_ref.dtype)


def mla_q(cq, gain, w_uq_perm, cos, sin, tm):
    m, ql = cq.shape
    n = w_uq_perm.shape[1]
    n_nope, n_rope = H_C * QK_NOPE, H_C * QK_ROPE
    tab_tiles = cos.shape[0] // tm
    tab_spec = pl.BlockSpec((tm, LANES), lambda i: (i % tab_tiles, 0))
    return pl.pallas_call(
        _mla_q_kernel,
        grid=(m // tm,),
        in_specs=[pl.BlockSpec((tm, ql), lambda i: (i, 0)), pl.BlockSpec((1, ql), lambda i: (0, 0)),
                  pl.BlockSpec((ql, n), lambda i: (0, 0)), tab_spec, tab_spec],
        out_specs=[pl.BlockSpec((tm, n_nope), lambda i: (i, 0)), pl.BlockSpec((tm, n_rope), lambda i: (i, 0))],
        out_shape=[jax.ShapeDtypeStruct((m, n_nope), BF16), jax.ShapeDtypeStruct((m, n_rope), BF16)],
        compiler_params=_cparams("parallel"),
        name="mla_q",
    )(cq, gain.reshape(1, ql), w_uq_perm, cos, sin)


def _mla_kv_kernel(*refs, with_heads):
    ckv_ref, kr_ref, g_ref = refs[:3]
    if with_heads:
        wk_ref, wvt_ref, cos_ref, sin_ref, lat_ref, rope_ref, kn_ref, vt_ref, kr16_ref = refs[3:]
    else:
        cos_ref, sin_ref, lat_ref, rope_ref = refs[3:]
    lat = _rms(ckv_ref[...], g_ref[...])
    lat_ref[...] = lat
    rope = _rope_lanes(kr_ref[...], cos_ref[...], sin_ref[...])
    rope_ref[...] = rope
    if with_heads:
        nt_dims = (((1,), (1,)), ((), ()))
        lb = lat.astype(BF16)
        kn_ref[...] = jnp.dot(lb, wk_ref[...], preferred_element_type=F32).astype(kn_ref.dtype)
        vt_ref[0] = lax.dot_general(wvt_ref[...], lb, nt_dims, preferred_element_type=F32).astype(vt_ref.dtype)
        kr16_ref[...] = rope.astype(kr16_ref.dtype)


def mla_kv(ckv, kr_rep, gain, w_uk, w_uv_t, cos, sin, tm, seq):
    m, c = ckv.shape
    with_heads = w_uk is not None
    tab_tiles = cos.shape[0] // tm
    tab_spec = pl.BlockSpec((tm, LANES), lambda i: (i % tab_tiles, 0))
    row = lambda wd: pl.BlockSpec((tm, wd), lambda i: (i, 0))
    args = [ckv, kr_rep, gain.reshape(1, c)]
    in_specs = [row(c), row(LANES), pl.BlockSpec((1, c), lambda i: (0, 0))]
    out_specs = [row(c), row(LANES)]
    out_shape = [jax.ShapeDtypeStruct((m, c), F32), jax.ShapeDtypeStruct((m, LANES), F32)]
    if with_heads:
        b, l = seq
        n = w_uk.shape[1]
        tiles = l // tm
        args += [w_uk, w_uv_t]
        in_specs += [pl.BlockSpec((c, n), lambda i: (0, 0)), pl.BlockSpec((n, c), lambda i: (0, 0))]
        out_specs += [row(n), pl.BlockSpec((1, n, tm), lambda i: (i // tiles, 0, i % tiles)), row(LANES)]
        out_shape += [jax.ShapeDtypeStruct((m, n), BF16), jax.ShapeDtypeStruct((b, n, l), BF16),
                      jax.ShapeDtypeStruct((m, LANES), BF16)]
    args += [cos, sin]
    in_specs += [tab_spec, tab_spec]
    return pl.pallas_call(
        functools.partial(_mla_kv_kernel, with_heads=with_heads),
        grid=(m // tm,),
        in_specs=in_specs,
        out_specs=out_specs,
        out_shape=out_shape,
        compiler_params=_cparams("parallel"),
        name="mla_kv",
    )(*args)


NEG_BIG = -1e30


def _mla_prompt_kernel(qn_ref, qr_ref, kn_ref, kr_ref, vt_ref, o_ref, *, blk, unrolls):
    p = pl.program_id(1)
    qi = pl.program_id(2)
    nt_dims = (((1,), (1,)), ((), ()))
    lane = lax.broadcasted_iota(jnp.int32, (1, LANES), 1)
    qn = qn_ref[0]
    qr = qr_ref[0]
    zero = jnp.zeros_like(qn)
    rows = []
    for hh in range(2):
        nope_lanes = (lane >> 6) == hh
        rope_lanes = (lane >> 5) == lax.rem(2 * p + hh, 4)
        rows.append(jnp.concatenate([jnp.where(nope_lanes, qn, zero), jnp.where(rope_lanes, qr, zero)], axis=1))
    qq = jnp.concatenate(rows, axis=0)
    dv = vt_ref.shape[1] // 2

    def logits(j):
        start = pl.multiple_of(j * blk, blk)
        kk = jnp.concatenate([kn_ref[0, pl.ds(start, blk), :], kr_ref[0, pl.ds(start, blk), :]], axis=1)
        return lax.dot_general(kk, qq, nt_dims, preferred_element_type=F32)

    def softmax_pv(j, st, carry, diag):
        m, l, acc0, acc1 = carry
        start = pl.multiple_of(j * blk, blk)
        if diag:
            kpos = lax.broadcasted_iota(jnp.int32, (blk, 2 * blk), 0)
            qpos = lax.broadcasted_iota(jnp.int32, (blk, 2 * blk), 1) & (blk - 1)
            st = jnp.where(kpos <= qpos, st, NEG_BIG)
        m_new = jnp.maximum(m, jnp.max(st, axis=0, keepdims=True))
        alpha = jnp.exp2(m - m_new)
        pr = jnp.exp2(st - m_new)
        l = l * alpha + jnp.sum(pr, axis=0, keepdims=True)
        pb = pr.astype(BF16)
        vt = vt_ref[0, :, pl.ds(start, blk)]
        acc0 = acc0 * alpha[:, :blk] + jnp.dot(vt[:dv], pb[:, :blk], preferred_element_type=F32)
        acc1 = acc1 * alpha[:, blk:] + jnp.dot(vt[dv:], pb[:, blk:], preferred_element_type=F32)
        return m_new, l, acc0, acc1

    init = (jnp.full((1, 2 * blk), NEG_BIG, F32), jnp.zeros((1, 2 * blk), F32),
            jnp.zeros((dv, blk), F32), jnp.zeros((dv, blk), F32))

    def stage(first, n_blocks, carry):
        sts = [logits(first + u) for u in range(n_blocks)]
        for u in range(n_blocks):
            carry = softmax_pv(first + u, sts[u], carry, False)
        return carry

    big = unrolls[0]
    n_big = qi // big
    carry = lax.fori_loop(0, n_big, lambda g, cy: stage(g * big, big, cy), init)
    done = n_big * big
    for n_blocks in unrolls[1:]:
        take = ((qi - done) // n_blocks).astype(jnp.int32)
        carry = lax.fori_loop(0, take, lambda _, cy, first=done, nb=n_blocks: stage(first, nb, cy), carry)
        done = done + take * n_blocks
    _, l, acc0, acc1 = softmax_pv(qi, logits(qi), carry, True)
    ot = jnp.concatenate([acc0 / l[:, :blk], acc1 / l[:, blk:]], axis=0)
    o_ref[0] = ot.T.astype(o_ref.dtype)


def mla_prompt(qn, qr, kn, kr16, vt, *, blk, unrolls):
    b, l, w = qn.shape
    q_spec = pl.BlockSpec((1, blk, LANES), lambda i, p, t: (i, t, p))
    qr_spec = pl.BlockSpec((1, blk, LANES), lambda i, p, t: (i, t, p // 2))
    k_spec = pl.BlockSpec((1, l, LANES), lambda i, p, t: (i, 0, p))
    kr_spec = pl.BlockSpec((1, l, LANES), lambda i, p, t: (i, 0, 0))
    vt_spec = pl.BlockSpec((1, LANES, l), lambda i, p, t: (i, p, 0))
    return pl.pallas_call(
        functools.partial(_mla_prompt_kernel, blk=blk, unrolls=unrolls),
        grid=(b, w // LANES, l // blk),
        in_specs=[q_spec, qr_spec, k_spec, kr_spec, vt_spec],
        out_specs=q_spec,
        out_shape=jax.ShapeDtypeStruct((b, l, w), BF16),
        compiler_params=_cparams("parallel", "parallel", "arbitrary"),
        name="mla_prompt",
    )(qn, qr, kn, kr16, vt)


def _mla_decode_kernel(pt_ref, ql_ref, qr_ref, ln_ref, rn_ref, lc_ref, rc_ref, o_ref, lbuf, rbuf, sem,
                       *, group, n_chains, n_sub, layer):
    b = pl.program_id(0)
    n_pages = pt_ref.shape[1]
    n_groups = n_pages // group
    page = lbuf.shape[2]
    rows = ql_ref.shape[1]
    nt_dims = (((1,), (1,)), ((), ()))

    def copies(bi, g, slot):
        cps = []
        for u in range(group):
            idx = pt_ref[bi, g * group + u]
            cps.append(pltpu.make_async_copy(lc_ref.at[layer, idx], lbuf.at[slot, u], sem.at[0, slot]))
            cps.append(pltpu.make_async_copy(rc_ref.at[layer, idx], rbuf.at[slot, u], sem.at[1, slot]))
        return cps

    @pl.when(b == 0)
    def _():
        for cp in copies(0, 0, 0):
            cp.start()

    ql = ql_ref[0]
    qr = qr_ref[0]

    def attend(lat, s_rope, mask, carry):
        m, l, acc = carry
        latb = lat.astype(BF16)
        s = lax.dot_general(ql, latb, nt_dims, preferred_element_type=F32) + s_rope
        if mask is not None:
            s = jnp.where(mask, s, NEG_BIG)
        m_new = jnp.maximum(m, jnp.max(s, axis=-1, keepdims=True))
        alpha = jnp.exp2(m - m_new)
        pr = jnp.exp2(s - m_new)
        l = l * alpha + jnp.sum(pr, axis=-1, keepdims=True)
        acc = acc * alpha + jnp.dot(pr.astype(BF16), latb, preferred_element_type=F32)
        return m_new, l, acc

    n_pad = ln_ref.shape[1]
    tq = lax.broadcasted_iota(jnp.int32, (rows, n_pad), 0) // H_C
    mask_new = lax.broadcasted_iota(jnp.int32, (rows, n_pad), 1) <= tq
    init = (jnp.full((rows, 1), NEG_BIG, F32), jnp.zeros((rows, 1), F32), jnp.zeros((rows, lbuf.shape[3]), F32))
    s_rope_new = lax.dot_general(qr, rn_ref[0].astype(BF16), nt_dims, preferred_element_type=F32)
    first = attend(ln_ref[0], s_rope_new, mask_new, init)

    per = group // (n_chains * n_sub)

    def body(g, carries):
        slot = lax.rem(b * n_groups + g, 2)

        @pl.when(g + 1 < n_groups)
        def _():
            for cp in copies(b, g + 1, 1 - slot):
                cp.start()

        @pl.when(jnp.logical_and(g + 1 == n_groups, b + 1 < pl.num_programs(0)))
        def _():
            for cp in copies(b + 1, 0, 1 - slot):
                cp.start()

        for cp in copies(b, g, slot):
            cp.wait()
        out = []
        for ci in range(n_chains):
            cy = carries[ci]
            for si in range(n_sub):
                u0 = (ci * n_sub + si) * per
                lat = lbuf[slot, u0:u0 + per].reshape(per * page, lbuf.shape[3])
                s_rope = [jnp.dot(qr, rbuf[slot, u].astype(BF16), preferred_element_type=F32)
                          for u in range(u0, u0 + per)]
                s_rope = s_rope[0] if per == 1 else jnp.concatenate(s_rope, axis=1)
                cy = attend(lat, s_rope, None, cy)
            out.append(cy)
        return tuple(out)

    carries = lax.fori_loop(0, n_groups, body, (first,) + (init,) * (n_chains - 1))
    m = carries[0][0]
    for cy in carries[1:]:
        m = jnp.maximum(m, cy[0])
    l = sum(cy[1] * jnp.exp2(cy[0] - m) for cy in carries)
    acc = sum(cy[2] * jnp.exp2(cy[0] - m) for cy in carries)
    o_ref[0] = acc / l


def mla_decode(q_lat, q_rope, lat_new, rope_new, lat_cache, rope_cache, page_table, *, layer, group,
               n_chains=1, n_sub=1):
    b, rows, c = q_lat.shape
    r = q_rope.shape[2]
    page = lat_cache.shape[2]
    n_pad = lat_new.shape[1]
    assert group % (n_chains * n_sub) == 0
    grid_spec = pltpu.PrefetchScalarGridSpec(
        num_scalar_prefetch=1,
        grid=(b,),
        in_specs=[pl.BlockSpec((1, rows, c), lambda i, pt: (i, 0, 0)),
                  pl.BlockSpec((1, rows, r), lambda i, pt: (i, 0, 0)),
                  pl.BlockSpec((1, n_pad, c), lambda i, pt: (i, 0, 0)),
                  pl.BlockSpec((1, n_pad, r), lambda i, pt: (i, 0, 0)),
                  pl.BlockSpec(memory_space=pl.ANY),
                  pl.BlockSpec(memory_space=pl.ANY)],
        out_specs=pl.BlockSpec((1, rows, c), lambda i, pt: (i, 0, 0)),
        scratch_shapes=[pltpu.VMEM((2, group, page, c), F32), pltpu.VMEM((2, group, r, page), F32),
                        pltpu.SemaphoreType.DMA((2, 2))],
    )
    return pl.pallas_call(
        functools.partial(_mla_decode_kernel, group=group, n_chains=n_chains, n_sub=n_sub, layer=layer),
        grid_spec=grid_spec,
        out_shape=jax.ShapeDtypeStruct((b, rows, c), F32),
        compiler_params=_cparams("arbitrary"),
        name="mla_decode",
    )(page_table, q_lat, q_rope, lat_new, rope_new, lat_cache, rope_cache)


TM = 512
TM_FF = 1024
TC_FF = 512
TN_FF = 1408
HGRN_CHUNK = 64
HGRN_SUB = 16
SB_BLK = 128
SB_WIN = 512
MLA_BLK = 256
MLA_UNROLLS = (8, 4, 2, 1)
MLA_PAGE_GROUP = 32
MLA_DECODE_CHAINS = 2


def _row_tile(m):
    return min(TM, m)


def _rope_tables(pos):
    half = QK_ROPE // 2
    freqs = jnp.exp(-math.log(ROPE_THETA) * jnp.arange(half, dtype=F32) * (2.0 / QK_ROPE))
    ang = pos.astype(F32)[:, None] * freqs[None, :]
    cos, sin = jnp.cos(ang), jnp.sin(ang)
    reps = LANES // QK_ROPE
    return (jnp.tile(jnp.concatenate([cos, cos], axis=1), (1, reps)),
            jnp.tile(jnp.concatenate([-sin, sin], axis=1), (1, reps)))


def _even_mixer(x, seq, w_in, w_out, lb, gnorm, gain, *, s0=None, caches=None, page_table=None, layer=0):
    b, l = seq
    wa, wb = H_A * DK_A, H_B * DH_B
    outs = ((0, wa, F32), (wa, wa, F32), (2 * wa, wa, F32), (3 * wa, wa, F32),
            (4 * wa, wb, BF16), (4 * wa + wb, wb, F32), (4 * wa + 2 * wb, wb, F32))
    if caches is None:
        outs += ((4 * wa + wb, wb, BF16), (4 * wa + 2 * wb, wb, BF16))
    tm = _row_tile(b * l)
    res = norm_proj(x, gain, w_in, outs, tm)
    qa, fa, ia, ga, qb, kb, vb = res[:7]
    r3 = lambda t: t.reshape(b, l, t.shape[-1])
    if caches is None:
        oa, state = hgrn2(r3(qa), r3(fa), r3(ia), r3(ga), lb, gnorm, None, chunk=HGRN_CHUNK, sub=HGRN_SUB)
        ob = sb_prompt(r3(qb), r3(res[7]), r3(res[8]), blk=SB_BLK, win=min(SB_WIN, l))
    else:
        oa, state = hgrn2(r3(qa), r3(fa), r3(ia), r3(ga), lb, gnorm, s0, chunk=l, sub=l)
        k_cache, v_cache = caches
        pad = ((0, 0), (0, 8 - l), (0, 0))
        by_pos = lambda c: jnp.transpose(c, (0, 1, 3, 4, 2)).reshape(c.shape[:2] + (wb, c.shape[2]))
        ob = sb_decode(jnp.tile(r3(qb), (1, H_B, 1)), jnp.pad(r3(kb), pad), jnp.pad(r3(vb), pad),
                       by_pos(k_cache), by_pos(v_cache), page_table, layer=layer, n_new=l)
    x = proj_res([oa.reshape(b * l, wa), ob.reshape(b * l, wb)], w_out, x, tm)
    return x, state, kb.reshape(b, l, H_B, DH_B), vb.reshape(b, l, H_B, DH_B)


def _odd_mixer(x, seq, pos, w_in_ext, gain, qn_g, kvn_g, w_uq_perm, w_uk, w_uv, w_out,
               *, caches=None, page_table=None, layer=0):
    b, l = seq
    m = b * l
    tm = _row_tile(m)
    cq, ckv, kr_rep = norm_proj(x, gain, w_in_ext,
                                ((0, Q_LORA, F32), (Q_LORA, KV_LORA, F32), (Q_LORA + KV_LORA, LANES, F32)), tm)
    cos, sin = _rope_tables(pos)
    qn, qr = mla_q(cq, qn_g, w_uq_perm, cos, sin, tm)
    if caches is None:
        w_uk2 = w_uk.reshape(KV_LORA, -1).astype(BF16)
        w_uv_t = w_uv.reshape(KV_LORA, -1).T.astype(BF16)
        lat, rope, kn, vt, kr16 = mla_kv(ckv, kr_rep, kvn_g, w_uk2, w_uv_t, cos, sin, tm, seq)
        r3 = lambda t: t.reshape(b, l, t.shape[-1])
        o = mla_prompt(r3(qn), r3(qr), r3(kn), r3(kr16), vt, blk=min(MLA_BLK, l), unrolls=MLA_UNROLLS).reshape(m, -1)
    else:
        lat, rope = mla_kv(ckv, kr_rep, kvn_g, None, None, cos, sin, tm, seq)
        eye2 = jnp.eye(2, dtype=F32)
        wk = jnp.transpose(w_uk, (1, 2, 0)).reshape(H_C // 2, 2, QK_NOPE, KV_LORA)
        wk = jnp.einsum('pand,ab->panbd', wk, eye2).reshape(H_C // 2, 2 * QK_NOPE, 2 * KV_LORA).astype(BF16)
        wv = jnp.transpose(w_uv, (1, 0, 2)).reshape(H_C // 2, 2, KV_LORA, DV_C)
        wv = jnp.einsum('pacv,ab->pacbv', wv, eye2).reshape(H_C // 2, 2 * KV_LORA, 2 * DV_C).astype(BF16)
        q_lat = group_mm(qn, wk, BF16).reshape(b, l * H_C, KV_LORA)
        lat_cache, rope_cache = caches
        pad = ((0, 0), (0, 8 - l), (0, 0))
        group = math.gcd(MLA_PAGE_GROUP, page_table.shape[1])
        o_lat = mla_decode(q_lat, qr.reshape(b, l * H_C, QK_ROPE),
                           jnp.pad(lat.reshape(b, l, KV_LORA), pad),
                           jnp.pad(rope[:, :QK_ROPE].reshape(b, l, QK_ROPE), pad),
                           lat_cache, jnp.transpose(rope_cache, (0, 1, 3, 2)), page_table, layer=layer,
                           group=group, n_chains=math.gcd(MLA_DECODE_CHAINS, group))
        o = group_mm(o_lat.reshape(m, H_C * KV_LORA), wv, BF16)
    x = proj_res([o], w_out, x, tm)
    return x, lat.reshape(b, l, KV_LORA), rope[:, :QK_ROPE].reshape(b, l, QK_ROPE)


def kernel(x_prompt, x_sample, cache_sb_k, cache_sb_v, cache_mla_latent, cache_mla_rope, state_hgrn, state_conv, page_table, w_in_even, w_out_even, hgrn_lb, hgrn_gnorm, w_in_odd, mla_q_norm, mla_kv_norm, w_uq, w_uk, w_uv, w_out_odd, norm_mix, norm_ffn, w_ffn_in, ffn_conv_w, ffn_conv_b, w_ffn_out, norm_final):
    bp, lp, d = x_prompt.shape
    bs, ls, _ = x_sample.shape
    depth = norm_mix.shape[0]
    past_len = page_table.shape[1] * cache_sb_k.shape[2]
    pos_p = jnp.arange(lp)
    pos_s = jnp.tile(past_len + jnp.arange(ls), bs)
    lb_all = jnp.cumsum(jax.nn.softmax(hgrn_lb.astype(F32), axis=0), axis=0)
    xp = x_prompt.reshape(bp * lp, d)
    xs = x_sample.reshape(bs * ls, d)
    sbk_p, sbv_p, sbk_s, sbv_s, hg_p, hg_s = [], [], [], [], [], []
    lat_p, rop_p, lat_s, rop_s, cv_p, cv_s = [], [], [], [], [], []
    for layer in range(depth):
        j = layer // 2
        if layer % 2 == 0:
            w_in = w_in_even[j].astype(BF16)
            w_out = w_out_even[j].astype(BF16)
            xp, sp, kp, vp = _even_mixer(xp, (bp, lp), w_in, w_out, lb_all[j], hgrn_gnorm[j], norm_mix[layer])
            xs, ss, ks, vs = _even_mixer(xs, (bs, ls), w_in, w_out, lb_all[j], hgrn_gnorm[j], norm_mix[layer],
                                         s0=state_hgrn[j], caches=(cache_sb_k, cache_sb_v),
                                         page_table=page_table, layer=j)
            sbk_p.append(kp), sbv_p.append(vp), sbk_s.append(ks), sbv_s.append(vs)
            hg_p.append(sp), hg_s.append(ss)
        else:
            split = Q_LORA + KV_LORA
            w_in = jnp.concatenate([w_in_odd[j][:, :split], jnp.tile(w_in_odd[j][:, split:], (1, LANES // QK_ROPE))],
                                   axis=1).astype(BF16)
            wq = w_uq[j].reshape(Q_LORA, H_C, QK_NOPE + QK_ROPE)
            w_uq_perm = jnp.concatenate([wq[:, :, :QK_NOPE].reshape(Q_LORA, -1), wq[:, :, QK_NOPE:].reshape(Q_LORA, -1)],
                                        axis=1).astype(BF16)
            w_out = w_out_odd[j].astype(BF16)
            common = (mla_q_norm[j], mla_kv_norm[j], w_uq_perm, w_uk[j], w_uv[j], w_out)
            xp, cp, rp = _odd_mixer(xp, (bp, lp), pos_p, w_in, norm_mix[layer], *common)
            xs, cs, rs = _odd_mixer(xs, (bs, ls), pos_s, w_in, norm_mix[layer], *common,
                                    caches=(cache_mla_latent, cache_mla_rope), page_table=page_table, layer=j)
            lat_p.append(cp), rop_p.append(rp), lat_s.append(cs), rop_s.append(rs)
        final = norm_final if layer == depth - 1 else None
        ffn_w = (norm_ffn[layer], w_ffn_in[layer].astype(BF16), ffn_conv_w[layer], ffn_conv_b[layer],
                 w_ffn_out[layer].astype(BF16))
        tm_p = min(TM_FF, lp)
        tm_s = _row_tile(bs * ls)
        xp, ap = conv_ffn(xp, *ffn_w, seq_len=lp, tm=tm_p, tc=min(TC_FF, tm_p), tn=TN_FF, final_gain=final)
        xs, as_ = conv_ffn(xs, *ffn_w, seq_len=ls, tm=tm_s, tc=min(TC_FF, tm_s), tn=TN_FF,
                           state=state_conv[layer], final_gain=final)
        d_ff = ap.shape[-1]
        cv_p.append(ap.reshape(bp, lp // tm_p, 8, d_ff)[:, -1, 8 - (CONV_W - 1):])
        cv_s.append(as_.reshape(bs, ls, d_ff)[:, ls - (CONV_W - 1):])
    return (xp.reshape(bp, lp, d), xs.reshape(bs, ls, d),
            jnp.stack(sbk_p), jnp.stack(sbv_p), jnp.stack(sbk_s), jnp.stack(sbv_s),
            jnp.stack(hg_p), jnp.stack(hg_s),
            jnp.stack(lat_p), jnp.stack(rop_p), jnp.stack(lat_s), jnp.stack(rop_s),
            jnp.stack(cv_p), jnp.stack(cv_s))
```

```python
import functools
import math

import jax
import jax.numpy as jnp
from jax import lax
from jax.experimental import pallas as pl
from jax.experimental.pallas import tpu as pltpu

F32 = jnp.float32
BF16 = jnp.bfloat16

EPS = 1e-6
ROPE_THETA = 10000.0

H_A, DK_A, DV_A = 4, 128, 128
H_B, DH_B = 8, 64
H_C, QK_NOPE, QK_ROPE, DV_C = 16, 64, 32, 64
KV_LORA, Q_LORA = 256, 384
CONV_W = 3

VMEM_LIMIT_BYTES = 56 * 1024 * 1024
LANES = 128
BF16_ROWS = 16

SB_LOG_ZERO = -104.0


def _cparams(*sem):
    return pltpu.CompilerParams(dimension_semantics=sem, vmem_limit_bytes=VMEM_LIMIT_BYTES)


def _rms(x, g):
    return x * lax.rsqrt(jnp.mean(x * x, axis=-1, keepdims=True) + EPS) * g


def _split3(x):
    hi = x.astype(BF16)
    r = x - hi.astype(F32)
    mid = r.astype(BF16)
    lo = (r - mid.astype(F32)).astype(BF16)
    return hi, mid, lo


def _softplus(z):
    return jnp.maximum(z, 0.0) + jnp.log1p(jnp.exp(-jnp.abs(z)))


def _norm_proj_kernel(x_ref, g_ref, w_ref, *out_refs, outs):
    h = _rms(x_ref[...], g_ref[...]).astype(BF16)
    for o_ref, (off, width, _) in zip(out_refs, outs):
        o_ref[...] = jnp.dot(h, w_ref[:, off:off + width], preferred_element_type=F32).astype(o_ref.dtype)


def norm_proj(x, gain, w, outs, tm):
    m, d = x.shape
    n = w.shape[1]
    return pl.pallas_call(
        functools.partial(_norm_proj_kernel, outs=outs),
        grid=(m // tm,),
        in_specs=[pl.BlockSpec((tm, d), lambda i: (i, 0)),
                  pl.BlockSpec((1, d), lambda i: (0, 0)),
                  pl.BlockSpec((d, n), lambda i: (0, 0))],
        out_specs=[pl.BlockSpec((tm, wd), lambda i: (i, 0)) for _, wd, _ in outs],
        out_shape=[jax.ShapeDtypeStruct((m, wd), dt) for _, wd, dt in outs],
        compiler_params=_cparams("parallel"),
        name="norm_proj",
    )(x, gain.reshape(1, d), w)


def _proj_res_kernel(*refs, n_in):
    a_refs = refs[:n_in]
    w_ref, x_ref, o_ref = refs[n_in:]
    acc = x_ref[...]
    off = 0
    for a_ref in a_refs:
        k = a_ref.shape[1]
        acc = acc + jnp.dot(a_ref[...], w_ref[off:off + k, :], preferred_element_type=F32)
        off += k
    o_ref[...] = acc


def proj_res(a_list, w, x, tm):
    m, d = x.shape
    k = w.shape[0]
    return pl.pallas_call(
        functools.partial(_proj_res_kernel, n_in=len(a_list)),
        grid=(m // tm,),
        in_specs=[pl.BlockSpec((tm, a.shape[1]), lambda i: (i, 0)) for a in a_list]
        + [pl.BlockSpec((k, d), lambda i: (0, 0)), pl.BlockSpec((tm, d), lambda i: (i, 0))],
        out_specs=pl.BlockSpec((tm, d), lambda i: (i, 0)),
        out_shape=jax.ShapeDtypeStruct((m, d), F32),
        compiler_params=_cparams("parallel"),
        name="proj_res",
    )(*a_list, w, x)


def _group_mm_kernel(x_ref, w_ref, o_ref):
    o_ref[...] = jnp.dot(x_ref[...].astype(BF16), w_ref[0], preferred_element_type=F32).astype(o_ref.dtype)


def group_mm(x, w, out_dtype):
    m = x.shape[0]
    p, kin, kout = w.shape
    return pl.pallas_call(
        _group_mm_kernel,
        grid=(p,),
        in_specs=[pl.BlockSpec((m, kin), lambda i: (0, i)), pl.BlockSpec((1, kin, kout), lambda i: (i, 0, 0))],
        out_specs=pl.BlockSpec((m, kout), lambda i: (0, i)),
        out_shape=jax.ShapeDtypeStruct((m, p * kout), out_dtype),
        compiler_params=_cparams("parallel"),
        name="group_mm",
    )(x, w)


def _ffn_kernel(*refs, tm, tc, seq_len, has_state, final_norm, a_rows):
    it = iter(refs)
    x_ref, xp_ref, g_ref, wa_ref, wb_ref, cw_ref, cb_ref, wo_ref = (next(it) for _ in range(8))
    e1_ref = next(it) if has_state else None
    e2_ref = next(it) if has_state else None
    gf_ref = next(it) if final_norm else None
    y_ref, a_ref = next(it), next(it)
    h_scr, acc_scr, a_scr = next(it), next(it), next(it)
    i = pl.program_id(0)
    j = pl.program_id(1)
    pad = BF16_ROWS

    @pl.when(j == 0)
    def _():
        h_scr[0:pad, :] = _rms(xp_ref[...], g_ref[...]).astype(BF16)
        h_scr[pad:, :] = _rms(x_ref[...], g_ref[...]).astype(BF16)
        acc_scr[...] = jnp.zeros_like(acc_scr)

    cw = cw_ref[...]
    for r0 in range(0, tm, tc):
        a_scr[...] = jnp.dot(h_scr[r0:r0 + tc + pad, :], wa_ref[...], preferred_element_type=F32)
        b = jnp.dot(h_scr[pad + r0:pad + r0 + tc, :], wb_ref[...], preferred_element_type=F32)
        a0 = a_scr[pad:, :]
        a1 = a_scr[pad - 1:pad - 1 + tc, :]
        a2 = a_scr[pad - 2:pad - 2 + tc, :]
        t = lax.rem(i * tm + r0 + lax.broadcasted_iota(jnp.int32, (tc, 1), 0), seq_len)
        if has_state:
            s1 = jnp.where(t >= 1, a1, e1_ref[r0:r0 + tc, :])
            s2 = jnp.where(t >= 2, a2, e2_ref[r0:r0 + tc, :])
        else:
            s1 = jnp.where(t >= 1, a1, 0.0)
            s2 = jnp.where(t >= 2, a2, 0.0)
        c = cb_ref[...] + (cw[0:1, :] * s2 + cw[1:2, :] * s1 + cw[2:3, :] * a0)
        gate = (c * jax.nn.sigmoid(c)) * b
        acc_scr[r0:r0 + tc, :] += jnp.dot(gate.astype(BF16), wo_ref[...], preferred_element_type=F32)
        if has_state:
            a_ref[r0:r0 + tc, :] = a0
        elif r0 + tc == tm:
            a_ref[...] = a_scr[pad + tc - a_rows:, :]

    @pl.when(j == pl.num_programs(1) - 1)
    def _():
        y = x_ref[...] + acc_scr[...]
        if final_norm:
            y = _rms(y, gf_ref[...])
        y_ref[...] = y


def conv_ffn(x, gain, w_in, conv_w, conv_b, w_out, *, seq_len, tm, tc, tn, state=None, final_gain=None):
    m, d = x.shape
    d_ff = w_out.shape[0]
    has_state = state is not None
    final_norm = final_gain is not None
    a_rows = tm if has_state else 8
    pad = BF16_ROWS
    nj = d_ff // tn
    tpb = tm // pad
    args = [x, x, gain.reshape(1, d), w_in, w_in, conv_w, conv_b.reshape(1, d_ff), w_out]
    in_specs = [
        pl.BlockSpec((tm, d), lambda i, j: (i, 0)),
        pl.BlockSpec((pad, d), lambda i, j: (jnp.maximum(i * tpb - 1, 0), 0)),
        pl.BlockSpec((1, d), lambda i, j: (0, 0)),
        pl.BlockSpec((d, tn), lambda i, j: (0, j)),
        pl.BlockSpec((d, tn), lambda i, j: (0, nj + j)),
        pl.BlockSpec((CONV_W, tn), lambda i, j: (0, j)),
        pl.BlockSpec((1, tn), lambda i, j: (0, j)),
        pl.BlockSpec((tn, d), lambda i, j: (j, 0)),
    ]
    if has_state:
        b = state.shape[0]
        e1 = jnp.broadcast_to(state[:, 1:2, :], (b, seq_len, d_ff)).reshape(m, d_ff)
        e2 = jnp.tile(state, (1, seq_len // 2, 1)).reshape(m, d_ff)
        args += [e1, e2]
        in_specs += [pl.BlockSpec((tm, tn), lambda i, j: (i, j))] * 2
    if final_norm:
        args.append(final_gain.reshape(1, d))
        in_specs.append(pl.BlockSpec((1, d), lambda i, j: (0, 0)))
    y, a_tail = pl.pallas_call(
        functools.partial(_ffn_kernel, tm=tm, tc=tc, seq_len=seq_len, has_state=has_state,
                          final_norm=final_norm, a_rows=a_rows),
        grid=(m // tm, nj),
        in_specs=in_specs,
        out_specs=[pl.BlockSpec((tm, d), lambda i, j: (i, 0)),
                   pl.BlockSpec((a_rows, tn), lambda i, j: (i, j))],
        out_shape=[jax.ShapeDtypeStruct((m, d), F32),
                   jax.ShapeDtypeStruct((m // tm * a_rows, d_ff), F32)],
        scratch_shapes=[pltpu.VMEM((tm + pad, d), BF16),
                        pltpu.VMEM((tm, d), F32),
                        pltpu.VMEM((tc + pad, tn), F32)],
        compiler_params=_cparams("parallel", "arbitrary"),
        name="conv_ffn",
    )(*args)
    return y, a_tail


def _hgrn_kernel(*refs, chunk, sub, has_s0):
    it = iter(refs)
    q_ref, f_ref, v_ref, g_ref, lb_ref, gn_ref = (next(it) for _ in range(6))
    s0_ref = next(it) if has_s0 else None
    o_ref, s_ref, st_scr = next(it), next(it), next(it)
    c = pl.program_id(1)
    n_sub = chunk // sub

    @pl.when(c == 0)
    def _():
        for h in range(H_A):
            if has_s0:
                st_scr[h] = s0_ref[0, h].T
            else:
                st_scr[h] = jnp.zeros((DV_A, DK_A), F32)

    row = lax.broadcasted_iota(jnp.int32, (chunk, chunk), 0)
    col = lax.broadcasted_iota(jnp.int32, (chunk, chunk), 1)
    tril = jnp.where(col <= row, 1.0, 0.0).astype(BF16)
    tt = lax.broadcasted_iota(jnp.int32, (sub, 1), 0)
    nt_dims = (((1,), (1,)), ((), ()))
    tn_dims = (((0,), (0,)), ((), ()))

    for h in range(H_A):
        sl = slice(h * DK_A, (h + 1) * DK_A)
        q = q_ref[0, :, sl]
        v = v_ref[0, :, sl]
        lb = lb_ref[:, sl]
        f = lb + (1.0 - lb) * jax.nn.sigmoid(f_ref[0, :, sl])
        kk = 1.0 - f
        logf = jnp.log(f)
        bc = sum(jnp.dot(tril, part, preferred_element_type=F32) for part in _split3(logf))
        st = st_scr[h]
        o = lax.dot_general((q * jnp.exp(bc)).astype(BF16), st.astype(BF16), nt_dims,
                            preferred_element_type=F32)
        vb = v.astype(BF16)
        parts = []
        for i in range(n_sub):
            r0 = i * sub
            qi, ki, vi, bi = q[r0:r0 + sub], kk[r0:r0 + sub], v[r0:r0 + sub], bc[r0:r0 + sub]
            oi = o[r0:r0 + sub]
            for s in range(sub):
                p = qi * jnp.exp(bi - bi[s:s + 1]) * ki[s:s + 1]
                att = jnp.where(tt >= s, jnp.sum(p, axis=-1, keepdims=True), 0.0)
                oi = oi + att * vi[s:s + 1]
            if i > 0:
                e = bc[r0 - 1:r0]
                qt = (qi * jnp.exp(bi - e)).astype(BF16)
                kt = (kk[:r0] * jnp.exp(e - bc[:r0])).astype(BF16)
                att = lax.dot_general(qt, kt, nt_dims, preferred_element_type=F32)
                oi = oi + jnp.dot(att.astype(BF16), vb[:r0], preferred_element_type=F32)
            parts.append(oi)
        o = parts[0] if n_sub == 1 else jnp.concatenate(parts, axis=0)
        bl = bc[chunk - 1:chunk]
        kh = (kk * jnp.exp(bl - bc)).astype(BF16)
        st_scr[h] = st * jnp.exp(bl) + lax.dot_general(vb, kh, tn_dims, preferred_element_type=F32)
        on = _rms(o, gn_ref[...])
        ga = g_ref[0, :, sl]
        o_ref[0, :, sl] = (on * (ga * jax.nn.sigmoid(ga))).astype(o_ref.dtype)

    @pl.when(c == pl.num_programs(1) - 1)
    def _():
        for h in range(H_A):
            s_ref[0, h] = st_scr[h].T


def hgrn2(q, f, v, g, lb, gnorm, s0, *, chunk, sub):
    b, l, w = q.shape
    has_s0 = s0 is not None
    seq_spec = pl.BlockSpec((1, chunk, w), lambda i, c: (i, c, 0))
    st_spec = pl.BlockSpec((1, H_A, DK_A, DV_A), lambda i, c: (i, 0, 0, 0))
    args = [q, f, v, g, lb.reshape(1, w), gnorm.reshape(1, DV_A)]
    in_specs = [seq_spec] * 4 + [pl.BlockSpec((1, w), lambda i, c: (0, 0)),
                                 pl.BlockSpec((1, DV_A), lambda i, c: (0, 0))]
    if has_s0:
        args.append(s0)
        in_specs.append(st_spec)
    return pl.pallas_call(
        functools.partial(_hgrn_kernel, chunk=chunk, sub=sub, has_s0=has_s0),
        grid=(b, l // chunk),
        in_specs=in_specs,
        out_specs=[seq_spec, st_spec],
        out_shape=[jax.ShapeDtypeStruct((b, l, w), BF16),
                   jax.ShapeDtypeStruct((b, H_A, DK_A, DV_A), F32)],
        scratch_shapes=[pltpu.VMEM((H_A, DV_A, DK_A), F32)],
        compiler_params=_cparams("parallel", "arbitrary"),
        name="hgrn2",
    )(*args)


def _sb_block(qm, kb, vm, mask, carry, umat, scale, keys_on_lanes=False):
    nt_dims = (((1,), (1,)), ((), ()))
    if keys_on_lanes:
        z = jnp.dot(qm, kb, preferred_element_type=F32) * scale
    else:
        z = lax.dot_general(qm, kb, nt_dims, preferred_element_type=F32) * scale
    lneg = -_softplus(z)
    if mask is not None:
        lneg = jnp.where(mask, lneg, 0.0)
    sb = umat.shape[0]
    n_sub = z.shape[1] // sb
    subs = [lneg[:, i * sb:(i + 1) * sb] for i in range(n_sub)]
    locs = [sum(jnp.dot(part, umat, preferred_element_type=F32) for part in _split3(s)) for s in subs]
    between = [None] * n_sub
    for i in reversed(range(n_sub)):
        between[i] = locs[i] + carry
        carry = carry + locs[i][:, 0:1] + subs[i][:, 0:1]
    between = between[0] if n_sub == 1 else jnp.concatenate(between, axis=1)
    w = jnp.exp(z + lneg + between)
    if mask is not None:
        w = jnp.where(mask, w, 0.0)
    if keys_on_lanes:
        out = lax.dot_general(w.astype(BF16), vm, nt_dims, preferred_element_type=F32)
    else:
        out = jnp.dot(w.astype(BF16), vm, preferred_element_type=F32)
    return out, carry


def _strict_upper(n):
    r = lax.broadcasted_iota(jnp.int32, (n, n), 0)
    c = lax.broadcasted_iota(jnp.int32, (n, n), 1)
    return jnp.where(r > c, 1.0, 0.0).astype(BF16)


def _sb_prompt_kernel(q_ref, k_ref, v_ref, o_ref, *, blk, win):
    qi = pl.program_id(2)
    scale = DH_B ** -0.5
    q = q_ref[0]
    lane = lax.broadcasted_iota(jnp.int32, (1, 2 * DH_B), 1)
    head_lanes = [lane < DH_B, lane >= DH_B]
    qs = [jnp.where(m, q, jnp.zeros_like(q)) for m in head_lanes]
    umat = _strict_upper(blk)
    qpos = qi * blk + lax.broadcasted_iota(jnp.int32, (blk, 1), 0)
    kiota = lax.broadcasted_iota(jnp.int32, (1, win), 1)

    def cond(state):
        hi, c0, c1, _ = state
        live = jnp.maximum(jnp.max(c0), jnp.max(c1)) > SB_LOG_ZERO
        return jnp.logical_and(hi > 0, live)

    def body(state):
        hi, c0, c1, acc = state
        start = pl.multiple_of(jnp.maximum(hi - win, 0), blk)
        kb = k_ref[0, pl.ds(start, win), :]
        vb = v_ref[0, pl.ds(start, win), :]
        mask = (start + kiota) < jnp.minimum(qpos, hi)
        cs = [c0, c1]
        for h in range(2):
            vm = jnp.where(head_lanes[h], vb, jnp.zeros_like(vb))
            out, cs[h] = _sb_block(qs[h], kb, vm, mask, cs[h], umat, scale)
            acc = acc + out
        return start, cs[0], cs[1], acc

    zero_c = jnp.zeros((blk, 1), F32)
    _, _, _, acc = lax.while_loop(cond, body, ((qi + 1) * blk, zero_c, zero_c, jnp.zeros((blk, 2 * DH_B), F32)))
    o_ref[0] = acc.astype(o_ref.dtype)


def sb_prompt(q, k, v, *, blk, win):
    b, l, w = q.shape
    pw = 2 * DH_B
    q_spec = pl.BlockSpec((1, blk, pw), lambda i, p, t: (i, t, p))
    kv_spec = pl.BlockSpec((1, l, pw), lambda i, p, t: (i, 0, p))
    return pl.pallas_call(
        functools.partial(_sb_prompt_kernel, blk=blk, win=win),
        grid=(b, w // pw, l // blk),
        in_specs=[q_spec, kv_spec, kv_spec],
        out_specs=q_spec,
        out_shape=jax.ShapeDtypeStruct((b, l, w), BF16),
        compiler_params=_cparams("parallel", "parallel", "arbitrary"),
        name="sb_prompt",
    )(q, k, v)


def _sb_decode_kernel(pt_ref, q_ref, kn_ref, vn_ref, kc_ref, vc_ref, o_ref, kbuf, vbuf, sem, *, n_new, layer):
    b = pl.program_id(0)
    n_pages = pt_ref.shape[1]
    page = kbuf.shape[2]
    w = q_ref.shape[2]
    rows = q_ref.shape[1]
    scale = DH_B ** -0.5

    def copies(j, slot):
        idx = pt_ref[b, j]
        return (pltpu.make_async_copy(kc_ref.at[layer, idx], kbuf.at[slot], sem.at[0, slot]),
                pltpu.make_async_copy(vc_ref.at[layer, idx], vbuf.at[slot], sem.at[1, slot]))

    for cp in copies(n_pages - 1, 0):
        cp.start()

    lane = lax.broadcasted_iota(jnp.int32, (rows, w), 1)
    rix = lax.broadcasted_iota(jnp.int32, (rows, w), 0)
    own = (lane // DH_B) == (rix // n_new)
    q = q_ref[0]
    qm = jnp.where(own, q, jnp.zeros_like(q))
    umat = _strict_upper(page)

    pad_rows = page - kn_ref.shape[1]
    knew = jnp.concatenate([kn_ref[0], jnp.zeros((pad_rows, w), F32)], axis=0).astype(BF16)
    vnew = jnp.concatenate([vn_ref[0], jnp.zeros((pad_rows, w), F32)], axis=0).astype(BF16)
    tq = lax.rem(lax.broadcasted_iota(jnp.int32, (rows, 1), 0), n_new)
    mask_new = lax.broadcasted_iota(jnp.int32, (1, page), 1) < tq
    acc, carry = _sb_block(qm, knew, vnew, mask_new, jnp.zeros((rows, 1), F32), umat, scale)

    def cond(state):
        j, _, carry, _ = state
        return jnp.logical_and(j >= 0, jnp.max(carry) > SB_LOG_ZERO)

    def body(state):
        j, slot, carry, acc = state

        @pl.when(j > 0)
        def _():
            for cp in copies(j - 1, 1 - slot):
                cp.start()

        for cp in copies(j, slot):
            cp.wait()
        out, carry = _sb_block(qm, kbuf[slot].astype(BF16), vbuf[slot].astype(BF16), None, carry, umat, scale,
                               keys_on_lanes=True)
        return j - 1, 1 - slot, carry, acc + out

    j, slot, _, acc = lax.while_loop(cond, body, (n_pages - 1, 0, carry, acc))

    @pl.when(j >= 0)
    def _():
        for cp in copies(j, slot):
            cp.wait()

    res = jnp.where(own, acc, 0.0)
    o = res[0:n_new]
    for h in range(1, rows // n_new):
        o = o + res[h * n_new:(h + 1) * n_new]
    o_ref[0] = o.astype(o_ref.dtype)


def sb_decode(q_rows, k_new, v_new, k_cache, v_cache, page_table, *, layer, n_new):
    b, rows, w = q_rows.shape
    page = k_cache.shape[3]
    grid_spec = pltpu.PrefetchScalarGridSpec(
        num_scalar_prefetch=1,
        grid=(b,),
        in_specs=[pl.BlockSpec((1, rows, w), lambda i, pt: (i, 0, 0)),
                  pl.BlockSpec((1, k_new.shape[1], w), lambda i, pt: (i, 0, 0)),
                  pl.BlockSpec((1, v_new.shape[1], w), lambda i, pt: (i, 0, 0)),
                  pl.BlockSpec(memory_space=pl.ANY),
                  pl.BlockSpec(memory_space=pl.ANY)],
        out_specs=pl.BlockSpec((1, n_new, w), lambda i, pt: (i, 0, 0)),
        scratch_shapes=[pltpu.VMEM((2, w, page), F32), pltpu.VMEM((2, w, page), F32),
                        pltpu.SemaphoreType.DMA((2, 2))],
    )
    return pl.pallas_call(
        functools.partial(_sb_decode_kernel, n_new=n_new, layer=layer),
        grid_spec=grid_spec,
        out_shape=jax.ShapeDtypeStruct((b, n_new, w), BF16),
        compiler_params=_cparams("arbitrary"),
        name="sb_decode",
    )(page_table, q_rows, k_new, v_new, k_cache, v_cache)


MLA_Q_SCALE = (QK_NOPE + QK_ROPE) ** -0.5 * math.log2(math.e)


def _rope_lanes(x, cos, sin_signed):
    w = x.shape[1]
    lane = lax.broadcasted_iota(jnp.int32, (1, w), 1)
    first = (lane & (QK_ROPE - 1)) < QK_ROPE // 2
    rot = jnp.where(first, pltpu.roll(x, w - QK_ROPE // 2, 1), pltpu.roll(x, QK_ROPE // 2, 1))
    return x * cos + rot * sin_signed


def _mla_q_kernel(cq_ref, g_ref, w_ref, cos_ref, sin_ref, qn_ref, qr_ref):
    n_nope = qn_ref.shape[1]
    h = _rms(cq_ref[...], g_ref[...]).astype(BF16)
    qn = jnp.dot(h, w_ref[:, :n_nope], preferred_element_type=F32)
    qn_ref[...] = (qn * MLA_Q_SCALE).astype(qn_ref.dtype)
    qr = jnp.dot(h, w_ref[:, n_nope:], preferred_element_type=F32)
    reps = qr.shape[1] // LANES
    cos = jnp.concatenate([cos_ref[...]] * reps, axis=1)
    sin = jnp.concatenate([sin_ref[...]] * reps, axis=1)
    qr_ref[...] = (_rope_lanes(qr, cos, sin) * MLA_Q_SCALE).astype(qr_ref.dtype)


def mla_q(cq, gain, w_uq_perm, cos, sin, tm):
    m, ql = cq.shape
    n = w_uq_perm.shape[1]
    n_nope, n_rope = H_C * QK_NOPE, H_C * QK_ROPE
    tab_tiles = cos.shape[0] // tm
    tab_spec = pl.BlockSpec((tm, LANES), lambda i: (i % tab_tiles, 0))
    return pl.pallas_call(
        _mla_q_kernel,
        grid=(m // tm,),
        in_specs=[pl.BlockSpec((tm, ql), lambda i: (i, 0)), pl.BlockSpec((1, ql), lambda i: (0, 0)),
                  pl.BlockSpec((ql, n), lambda i: (0, 0)), tab_spec, tab_spec],
        out_specs=[pl.BlockSpec((tm, n_nope), lambda i: (i, 0)), pl.BlockSpec((tm, n_rope), lambda i: (i, 0))],
        out_shape=[jax.ShapeDtypeStruct((m, n_nope), BF16), jax.ShapeDtypeStruct((m, n_rope), BF16)],
        compiler_params=_cparams("parallel"),
        name="mla_q",
    )(cq, gain.reshape(1, ql), w_uq_perm, cos, sin)


def _mla_kv_kernel(*refs, with_heads):
    ckv_ref, kr_ref, g_ref = refs[:3]
    if with_heads:
        wk_ref, wvt_ref, cos_ref, sin_ref, lat_ref, rope_ref, kn_ref, vt_ref, kr16_ref = refs[3:]
    else:
        cos_ref, sin_ref, lat_ref, rope_ref = refs[3:]
    lat = _rms(ckv_ref[...], g_ref[...])
    lat_ref[...] = lat
    rope = _rope_lanes(kr_ref[...], cos_ref[...], sin_ref[...])
    rope_ref[...] = rope
    if with_heads:
        nt_dims = (((1,), (1,)), ((), ()))
        lb = lat.astype(BF16)
        kn_ref[...] = jnp.dot(lb, wk_ref[...], preferred_element_type=F32).astype(kn_ref.dtype)
        vt_ref[0] = lax.dot_general(wvt_ref[...], lb, nt_dims, preferred_element_type=F32).astype(vt_ref.dtype)
        kr16_ref[...] = rope.astype(kr16_ref.dtype)


def mla_kv(ckv, kr_rep, gain, w_uk, w_uv_t, cos, sin, tm, seq):
    m, c = ckv.shape
    with_heads = w_uk is not None
    tab_tiles = cos.shape[0] // tm
    tab_spec = pl.BlockSpec((tm, LANES), lambda i: (i % tab_tiles, 0))
    row = lambda wd: pl.BlockSpec((tm, wd), lambda i: (i, 0))
    args = [ckv, kr_rep, gain.reshape(1, c)]
    in_specs = [row(c), row(LANES), pl.BlockSpec((1, c), lambda i: (0, 0))]
    out_specs = [row(c), row(LANES)]
    out_shape = [jax.ShapeDtypeStruct((m, c), F32), jax.ShapeDtypeStruct((m, LANES), F32)]
    if with_heads:
        b, l = seq
        n = w_uk.shape[1]
        tiles = l // tm
        args += [w_uk, w_uv_t]
        in_specs += [pl.BlockSpec((c, n), lambda i: (0, 0)), pl.BlockSpec((n, c), lambda i: (0, 0))]
        out_specs += [row(n), pl.BlockSpec((1, n, tm), lambda i: (i // tiles, 0, i % tiles)), row(LANES)]
        out_shape += [jax.ShapeDtypeStruct((m, n), BF16), jax.ShapeDtypeStruct((b, n, l), BF16),
                      jax.ShapeDtypeStruct((m, LANES), BF16)]
    args += [cos, sin]
    in_specs += [tab_spec, tab_spec]
    return pl.pallas_call(
        functools.partial(_mla_kv_kernel, with_heads=with_heads),
        grid=(m // tm,),
        in_specs=in_specs,
        out_specs=out_specs,
        out_shape=out_shape,
        compiler_params=_cparams("parallel"),
        name="mla_kv",
    )(*args)


NEG_BIG = -1e30


def _mla_prompt_kernel(qn_ref, qr_ref, kn_ref, kr_ref, vt_ref, o_ref, *, blk, unrolls):
    p = pl.program_id(1)
    qi = pl.program_id(2)
    nt_dims = (((1,), (1,)), ((), ()))
    lane = lax.broadcasted_iota(jnp.int32, (1, LANES), 1)
    qn = qn_ref[0]
    qr = qr_ref[0]
    zero = jnp.zeros_like(qn)
    rows = []
    for hh in range(2):
        nope_lanes = (lane >> 6) == hh
        rope_lanes = (lane >> 5) == lax.rem(2 * p + hh, 4)
        rows.append(jnp.concatenate([jnp.where(nope_lanes, qn, zero), jnp.where(rope_lanes, qr, zero)], axis=1))
    qq = jnp.concatenate(rows, axis=0)
    dv = vt_ref.shape[1] // 2

    def logits(j):
        start = pl.multiple_of(j * blk, blk)
        kk = jnp.concatenate([kn_ref[0, pl.ds(start, blk), :], kr_ref[0, pl.ds(start, blk), :]], axis=1)
        return lax.dot_general(kk, qq, nt_dims, preferred_element_type=F32)

    def softmax_pv(j, st, carry, diag):
        m, l, acc0, acc1 = carry
        start = pl.multiple_of(j * blk, blk)
        if diag:
            kpos = lax.broadcasted_iota(jnp.int32, (blk, 2 * blk), 0)
            qpos = lax.broadcasted_iota(jnp.int32, (blk, 2 * blk), 1) & (blk - 1)
            st = jnp.where(kpos <= qpos, st, NEG_BIG)
        m_new = jnp.maximum(m, jnp.max(st, axis=0, keepdims=True))
        alpha = jnp.exp2(m - m_new)
        pr = jnp.exp2(st - m_new)
        l = l * alpha + jnp.sum(pr, axis=0, keepdims=True)
        pb = pr.astype(BF16)
        vt = vt_ref[0, :, pl.ds(start, blk)]
        acc0 = acc0 * alpha[:, :blk] + jnp.dot(vt[:dv], pb[:, :blk], preferred_element_type=F32)
        acc1 = acc1 * alpha[:, blk:] + jnp.dot(vt[dv:], pb[:, blk:], preferred_element_type=F32)
        return m_new, l, acc0, acc1

    init = (jnp.full((1, 2 * blk), NEG_BIG, F32), jnp.zeros((1, 2 * blk), F32),
            jnp.zeros((dv, blk), F32), jnp.zeros((dv, blk), F32))

    def stage(first, n_blocks, carry):
        sts = [logits(first + u) for u in range(n_blocks)]
        for u in range(n_blocks):
            carry = softmax_pv(first + u, sts[u], carry, False)
        return carry

    big = unrolls[0]
    n_big = qi // big
    carry = lax.fori_loop(0, n_big, lambda g, cy: stage(g * big, big, cy), init)
    done = n_big * big
    for n_blocks in unrolls[1:]:
        take = ((qi - done) // n_blocks).astype(jnp.int32)
        carry = lax.fori_loop(0, take, lambda _, cy, first=done, nb=n_blocks: stage(first, nb, cy), carry)
        done = done + take * n_blocks
    _, l, acc0, acc1 = softmax_pv(qi, logits(qi), carry, True)
    ot = jnp.concatenate([acc0 / l[:, :blk], acc1 / l[:, blk:]], axis=0)
    o_ref[0] = ot.T.astype(o_ref.dtype)


def mla_prompt(qn, qr, kn, kr16, vt, *, blk, unrolls):
    b, l, w = qn.shape
    q_spec = pl.BlockSpec((1, blk, LANES), lambda i, p, t: (i, t, p))
    qr_spec = pl.BlockSpec((1, blk, LANES), lambda i, p, t: (i, t, p // 2))
    k_spec = pl.BlockSpec((1, l, LANES), lambda i, p, t: (i, 0, p))
    kr_spec = pl.BlockSpec((1, l, LANES), lambda i, p, t: (i, 0, 0))
    vt_spec = pl.BlockSpec((1, LANES, l), lambda i, p, t: (i, p, 0))
    return pl.pallas_call(
        functools.partial(_mla_prompt_kernel, blk=blk, unrolls=unrolls),
        grid=(b, w // LANES, l // blk),
        in_specs=[q_spec, qr_spec, k_spec, kr_spec, vt_spec],
        out_specs=q_spec,
        out_shape=jax.ShapeDtypeStruct((b, l, w), BF16),
        compiler_params=_cparams("parallel", "parallel", "arbitrary"),
        name="mla_prompt",
    )(qn, qr, kn, kr16, vt)


def _mla_decode_kernel(pt_ref, ql_ref, qr_ref, ln_ref, rn_ref, lc_ref, rc_ref, o_ref, lbuf, rbuf, sem,
                       *, group, n_chains, n_sub, layer):
    b = pl.program_id(0)
    n_pages = pt_ref.shape[1]
    n_groups = n_pages // group
    page = lbuf.shape[2]
    rows = ql_ref.shape[1]
    nt_dims = (((1,), (1,)), ((), ()))

    def copies(bi, g, slot):
        cps = []
        for u in range(group):
            idx = pt_ref[bi, g * group + u]
            cps.append(pltpu.make_async_copy(lc_ref.at[layer, idx], lbuf.at[slot, u], sem.at[0, slot]))
            cps.append(pltpu.make_async_copy(rc_ref.at[layer, idx], rbuf.at[slot, u], sem.at[1, slot]))
        return cps

    @pl.when(b == 0)
    def _():
        for cp in copies(0, 0, 0):
            cp.start()

    ql = ql_ref[0]
    qr = qr_ref[0]

    def attend(lat, s_rope, mask, carry):
        m, l, acc = carry
        latb = lat.astype(BF16)
        s = lax.dot_general(ql, latb, nt_dims, preferred_element_type=F32) + s_rope
        if mask is not None:
            s = jnp.where(mask, s, NEG_BIG)
        m_new = jnp.maximum(m, jnp.max(s, axis=-1, keepdims=True))
        alpha = jnp.exp2(m - m_new)
        pr = jnp.exp2(s - m_new)
        l = l * alpha + jnp.sum(pr, axis=-1, keepdims=True)
        acc = acc * alpha + jnp.dot(pr.astype(BF16), latb, preferred_element_type=F32)
        return m_new, l, acc

    n_pad = ln_ref.shape[1]
    tq = lax.broadcasted_iota(jnp.int32, (rows, n_pad), 0) // H_C
    mask_new = lax.broadcasted_iota(jnp.int32, (rows, n_pad), 1) <= tq
    init = (jnp.full((rows, 1), NEG_BIG, F32), jnp.zeros((rows, 1), F32), jnp.zeros((rows, lbuf.shape[3]), F32))
    s_rope_new = lax.dot_general(qr, rn_ref[0].astype(BF16), nt_dims, preferred_element_type=F32)
    first = attend(ln_ref[0], s_rope_new, mask_new, init)

    per = group // (n_chains * n_sub)

    def body(g, carries):
        slot = lax.rem(b * n_groups + g, 2)

        @pl.when(g + 1 < n_groups)
        def _():
            for cp in copies(b, g + 1, 1 - slot):
                cp.start()

        @pl.when(jnp.logical_and(g + 1 == n_groups, b + 1 < pl.num_programs(0)))
        def _():
            for cp in copies(b + 1, 0, 1 - slot):
                cp.start()

        for cp in copies(b, g, slot):
            cp.wait()
        out = []
        for ci in range(n_chains):
            cy = carries[ci]
            for si in range(n_sub):
                u0 = (ci * n_sub + si) * per
                lat = lbuf[slot, u0:u0 + per].reshape(per * page, lbuf.shape[3])
                s_rope = [jnp.dot(qr, rbuf[slot, u].astype(BF16), preferred_element_type=F32)
                          for u in range(u0, u0 + per)]
                s_rope = s_rope[0] if per == 1 else jnp.concatenate(s_rope, axis=1)
                cy = attend(lat, s_rope, None, cy)
            out.append(cy)
        return tuple(out)

    carries = lax.fori_loop(0, n_groups, body, (first,) + (init,) * (n_chains - 1))
    m = carries[0][0]
    for cy in carries[1:]:
        m = jnp.maximum(m, cy[0])
    l = sum(cy[1] * jnp.exp2(cy[0] - m) for cy in carries)
    acc = sum(cy[2] * jnp.exp2(cy[0] - m) for cy in carries)
    o_ref[0] = acc / l


def mla_decode(q_lat, q_rope, lat_new, rope_new, lat_cache, rope_cache, page_table, *, layer, group,
               n_chains=1, n_sub=1):
    b, rows, c = q_lat.shape
    r = q_rope.shape[2]
    page = lat_cache.shape[2]
    n_pad = lat_new.shape[1]
    assert group % (n_chains * n_sub) == 0
    grid_spec = pltpu.PrefetchScalarGridSpec(
        num_scalar_prefetch=1,
        grid=(b,),
        in_specs=[pl.BlockSpec((1, rows, c), lambda i, pt: (i, 0, 0)),
                  pl.BlockSpec((1, rows, r), lambda i, pt: (i, 0, 0)),
                  pl.BlockSpec((1, n_pad, c), lambda i, pt: (i, 0, 0)),
                  pl.BlockSpec((1, n_pad, r), lambda i, pt: (i, 0, 0)),
                  pl.BlockSpec(memory_space=pl.ANY),
                  pl.BlockSpec(memory_space=pl.ANY)],
        out_specs=pl.BlockSpec((1, rows, c), lambda i, pt: (i, 0, 0)),
        scratch_shapes=[pltpu.VMEM((2, group, page, c), F32), pltpu.VMEM((2, group, r, page), F32),
                        pltpu.SemaphoreType.DMA((2, 2))],
    )
    return pl.pallas_call(
        functools.partial(_mla_decode_kernel, group=group, n_chains=n_chains, n_sub=n_sub, layer=layer),
        grid_spec=grid_spec,
        out_shape=jax.ShapeDtypeStruct((b, rows, c), F32),
        compiler_params=_cparams("arbitrary"),
        name="mla_decode",
    )(page_table, q_lat, q_rope, lat_new, rope_new, lat_cache, rope_cache)


TM = 512
TM_FF = 1024
TC_FF = 512
TN_FF = 1408
HGRN_CHUNK = 64
HGRN_SUB = 16
SB_BLK = 128
SB_WIN = 512
MLA_BLK = 256
MLA_UNROLLS = (16, 8, 4, 2, 1)
MLA_PAGE_GROUP = 64
MLA_DECODE_CHAINS = 2


def _row_tile(m):
    return min(TM, m)


def _rope_tables(pos):
    half = QK_ROPE // 2
    freqs = jnp.exp(-math.log(ROPE_THETA) * jnp.arange(half, dtype=F32) * (2.0 / QK_ROPE))
    ang = pos.astype(F32)[:, None] * freqs[None, :]
    cos, sin = jnp.cos(ang), jnp.sin(ang)
    reps = LANES // QK_ROPE
    return (jnp.tile(jnp.concatenate([cos, cos], axis=1), (1, reps)),
            jnp.tile(jnp.concatenate([-sin, sin], axis=1), (1, reps)))


def _even_mixer(x, seq, w_in, w_out, lb, gnorm, gain, *, s0=None, caches=None, page_table=None, layer=0):
    b, l = seq
    wa, wb = H_A * DK_A, H_B * DH_B
    outs = ((0, wa, F32), (wa, wa, F32), (2 * wa, wa, F32), (3 * wa, wa, F32),
            (4 * wa, wb, BF16), (4 * wa + wb, wb, F32), (4 * wa + 2 * wb, wb, F32))
    if caches is None:
        outs += ((4 * wa + wb, wb, BF16), (4 * wa + 2 * wb, wb, BF16))
    tm = _row_tile(b * l)
    res = norm_proj(x, gain, w_in, outs, tm)
    qa, fa, ia, ga, qb, kb, vb = res[:7]
    r3 = lambda t: t.reshape(b, l, t.shape[-1])
    if caches is None:
        oa, state = hgrn2(r3(qa), r3(fa), r3(ia), r3(ga), lb, gnorm, None, chunk=HGRN_CHUNK, sub=HGRN_SUB)
        ob = sb_prompt(r3(qb), r3(res[7]), r3(res[8]), blk=SB_BLK, win=min(SB_WIN, l))
    else:
        oa, state = hgrn2(r3(qa), r3(fa), r3(ia), r3(ga), lb, gnorm, s0, chunk=l, sub=l)
        k_cache, v_cache = caches
        pad = ((0, 0), (0, 8 - l), (0, 0))
        by_pos = lambda c: jnp.transpose(c, (0, 1, 3, 4, 2)).reshape(c.shape[:2] + (wb, c.shape[2]))
        ob = sb_decode(jnp.tile(r3(qb), (1, H_B, 1)), jnp.pad(r3(kb), pad), jnp.pad(r3(vb), pad),
                       by_pos(k_cache), by_pos(v_cache), page_table, layer=layer, n_new=l)
    x = proj_res([oa.reshape(b * l, wa), ob.reshape(b * l, wb)], w_out, x, tm)
    return x, state, kb.reshape(b, l, H_B, DH_B), vb.reshape(b, l, H_B, DH_B)


def _odd_mixer(x, seq, pos, w_in_ext, gain, qn_g, kvn_g, w_uq_perm, w_uk, w_uv, w_out,
               *, caches=None, page_table=None, layer=0):
    b, l = seq
    m = b * l
    tm = _row_tile(m)
    cq, ckv, kr_rep = norm_proj(x, gain, w_in_ext,
                                ((0, Q_LORA, F32), (Q_LORA, KV_LORA, F32), (Q_LORA + KV_LORA, LANES, F32)), tm)
    cos, sin = _rope_tables(pos)
    qn, qr = mla_q(cq, qn_g, w_uq_perm, cos, sin, tm)
    if caches is None:
        w_uk2 = w_uk.reshape(KV_LORA, -1).astype(BF16)
        w_uv_t = w_uv.reshape(KV_LORA, -1).T.astype(BF16)
        lat, rope, kn, vt, kr16 = mla_kv(ckv, kr_rep, kvn_g, w_uk2, w_uv_t, cos, sin, tm, seq)
        r3 = lambda t: t.reshape(b, l, t.shape[-1])
        o = mla_prompt(r3(qn), r3(qr), r3(kn), r3(kr16), vt, blk=min(MLA_BLK, l), unrolls=MLA_UNROLLS).reshape(m, -1)
    else:
        lat, rope = mla_kv(ckv, kr_rep, kvn_g, None, None, cos, sin, tm, seq)
        eye2 = jnp.eye(2, dtype=F32)
        wk = jnp.transpose(w_uk, (1, 2, 0)).reshape(H_C // 2, 2, QK_NOPE, KV_LORA)
        wk = jnp.einsum('pand,ab->panbd', wk, eye2).reshape(H_C // 2, 2 * QK_NOPE, 2 * KV_LORA).astype(BF16)
        wv = jnp.transpose(w_uv, (1, 0, 2)).reshape(H_C // 2, 2, KV_LORA, DV_C)
        wv = jnp.einsum('pacv,ab->pacbv', wv, eye2).reshape(H_C // 2, 2 * KV_LORA, 2 * DV_C).astype(BF16)
        q_lat = group_mm(qn, wk, BF16).reshape(b, l * H_C, KV_LORA)
        lat_cache, rope_cache = caches
        pad = ((0, 0), (0, 8 - l), (0, 0))
        group = math.gcd(MLA_PAGE_GROUP, page_table.shape[1])
        o_lat = mla_decode(q_lat, qr.reshape(b, l * H_C, QK_ROPE),
                           jnp.pad(lat.reshape(b, l, KV_LORA), pad),
                           jnp.pad(rope[:, :QK_ROPE].reshape(b, l, QK_ROPE), pad),
                           lat_cache, jnp.transpose(rope_cache, (0, 1, 3, 2)), page_table, layer=layer,
                           group=group, n_chains=math.gcd(MLA_DECODE_CHAINS, group))
        o = group_mm(o_lat.reshape(m, H_C * KV_LORA), wv, BF16)
    x = proj_res([o], w_out, x, tm)
    return x, lat.reshape(b, l, KV_LORA), rope[:, :QK_ROPE].reshape(b, l, QK_ROPE)


def kernel(x_prompt, x_sample, cache_sb_k, cache_sb_v, cache_mla_latent, cache_mla_rope, state_hgrn, state_conv, page_table, w_in_even, w_out_even, hgrn_lb, hgrn_gnorm, w_in_odd, mla_q_norm, mla_kv_norm, w_uq, w_uk, w_uv, w_out_odd, norm_mix, norm_ffn, w_ffn_in, ffn_conv_w, ffn_conv_b, w_ffn_out, norm_final):
    bp, lp, d = x_prompt.shape
    bs, ls, _ = x_sample.shape
    depth = norm_mix.shape[0]
    past_len = page_table.shape[1] * cache_sb_k.shape[2]
    pos_p = jnp.arange(lp)
    pos_s = jnp.tile(past_len + jnp.arange(ls), bs)
    lb_all = jnp.cumsum(jax.nn.softmax(hgrn_lb.astype(F32), axis=0), axis=0)
    xp = x_prompt.reshape(bp * lp, d)
    xs = x_sample.reshape(bs * ls, d)
    sbk_p, sbv_p, sbk_s, sbv_s, hg_p, hg_s = [], [], [], [], [], []
    lat_p, rop_p, lat_s, rop_s, cv_p, cv_s = [], [], [], [], [], []
    for layer in range(depth):
        j = layer // 2
        if layer % 2 == 0:
            w_in = w_in_even[j].astype(BF16)
            w_out = w_out_even[j].astype(BF16)
            xp, sp, kp, vp = _even_mixer(xp, (bp, lp), w_in, w_out, lb_all[j], hgrn_gnorm[j], norm_mix[layer])
            xs, ss, ks, vs = _even_mixer(xs, (bs, ls), w_in, w_out, lb_all[j], hgrn_gnorm[j], norm_mix[layer],
                                         s0=state_hgrn[j], caches=(cache_sb_k, cache_sb_v),
                                         page_table=page_table, layer=j)
            sbk_p.append(kp), sbv_p.append(vp), sbk_s.append(ks), sbv_s.append(vs)
            hg_p.append(sp), hg_s.append(ss)
        else:
            split = Q_LORA + KV_LORA
            w_in = jnp.concatenate([w_in_odd[j][:, :split], jnp.tile(w_in_odd[j][:, split:], (1, LANES // QK_ROPE))],
                                   axis=1).astype(BF16)
            wq = w_uq[j].reshape(Q_LORA, H_C, QK_NOPE + QK_ROPE)
            w_uq_perm = jnp.concatenate([wq[:, :, :QK_NOPE].reshape(Q_LORA, -1), wq[:, :, QK_NOPE:].reshape(Q_LORA, -1)],
                                        axis=1).astype(BF16)
            w_out = w_out_odd[j].astype(BF16)
            common = (mla_q_norm[j], mla_kv_norm[j], w_uq_perm, w_uk[j], w_uv[j], w_out)
            xp, cp, rp = _odd_mixer(xp, (bp, lp), pos_p, w_in, norm_mix[layer], *common)
            xs, cs, rs = _odd_mixer(xs, (bs, ls), pos_s, w_in, norm_mix[layer], *common,
                                    caches=(cache_mla_latent, cache_mla_rope), page_table=page_table, layer=j)
            lat_p.append(cp), rop_p.append(rp), lat_s.append(cs), rop_s.append(rs)
        final = norm_final if layer == depth - 1 else None
        ffn_w = (norm_ffn[layer], w_ffn_in[layer].astype(BF16), ffn_conv_w[layer], ffn_conv_b[layer],
                 w_ffn_out[layer].astype(BF16))
        tm_p = min(TM_FF, lp)
        tm_s = _row_tile(bs * ls)
        xp, ap = conv_ffn(xp, *ffn_w, seq_len=lp, tm=tm_p, tc=min(TC_FF, tm_p), tn=TN_FF, final_gain=final)
        xs, as_ = conv_ffn(xs, *ffn_w, seq_len=ls, tm=tm_s, tc=min(TC_FF, tm_s), tn=TN_FF,
                           state=state_conv[layer], final_gain=final)
        d_ff = ap.shape[-1]
        cv_p.append(ap.reshape(bp, lp // tm_p, 8, d_ff)[:, -1, 8 - (CONV_W - 1):])
        cv_s.append(as_.reshape(bs, ls, d_ff)[:, ls - (CONV_W - 1):])
    return (xp.reshape(bp, lp, d), xs.reshape(bs, ls, d),
            jnp.stack(sbk_p), jnp.stack(sbv_p), jnp.stack(sbk_s), jnp.stack(sbv_s),
            jnp.stack(hg_p), jnp.stack(hg_s),
            jnp.stack(lat_p), jnp.stack(rop_p), jnp.stack(lat_s), jnp.stack(rop_s),
            jnp.stack(cv_p), jnp.stack(cv_s))
```
